```python
import math
import jax, jax.numpy as jnp
from jax import lax
import numpy as np

D_MODEL = 2048
BATCH = 2
SEQ = 4096
DEPTH = 1

GRID_W = 64
CTX_LEN = 256
EPS = 1e-6
ROPE_BASE = 10000.0
H_R = 8
R_DK = 128
R_DV = 256
RET_CHUNK = 128
H_D = 8
D_HEAD = 64
Q_BLOCK = 128
RK = H_R * R_DK
RV = H_R * R_DV
DKW = H_D * 2 * D_HEAD
DVW = H_D * 2 * D_HEAD
KV_COLS = RK + RV + DKW + DVW
Q_COLS = RK + RV + DKW + 2 * D_MODEL
IN_COLS = KV_COLS + Q_COLS
N_EXPERTS = 256
TOP_K = 8
N_GROUPS = 8
TOPK_GROUPS = 4
D_EXPERT = 256
D_SHARED = 256
ROUTED_SCALE = 2.5
MOE_BLOCK = 64

kernel_name = 'hybrid_retention_diffattn_moe_dit'


def rms_norm(x, w):
    xf = x.astype(jnp.float32)
    y = xf * lax.rsqrt(jnp.mean(xf * xf, axis=-1, keepdims=True) + EPS) * w.astype(jnp.float32)
    return y.astype(x.dtype)


def modulate(h, shift, scale):
    return h * (1.0 + scale) + shift


def rope_1d(x, pos):
    half = x.shape[-1] // 2
    inv = ROPE_BASE ** (-jnp.arange(half, dtype=jnp.float32) / half)
    ang = pos.astype(jnp.float32)[:, None] * inv
    shape = (1, pos.shape[0]) + (1,) * (x.ndim - 3) + (half,)
    cos = jnp.cos(ang).reshape(shape)
    sin = jnp.sin(ang).reshape(shape)
    xf = x.astype(jnp.float32)
    x1, x2 = xf[..., :half], xf[..., half:]
    return jnp.concatenate([x1 * cos - x2 * sin, x1 * sin + x2 * cos], axis=-1).astype(x.dtype)


def axial_rope(x, row, col):
    half = x.shape[-1] // 2
    return jnp.concatenate([rope_1d(x[..., :half], row), rope_1d(x[..., half:], col)], axis=-1)


def kv_heads(p):
    B, n = p.shape[:2]
    rk, rv, dk, dv = jnp.split(p[..., :KV_COLS], [RK, RK + RV, RK + RV + DKW], axis=-1)
    return (rk.reshape(B, n, H_R, R_DK), rv.reshape(B, n, H_R, R_DV),
            dk.reshape(B, n, H_D, 2, D_HEAD), dv.reshape(B, n, H_D, 2 * D_HEAD))


def q_heads(p):
    B, n = p.shape[:2]
    rq, rg, dq, gates = jnp.split(p[..., KV_COLS:], [RK, RK + RV, RK + RV + DKW], axis=-1)
    return rq.reshape(B, n, H_R, R_DK), rg, dq.reshape(B, n, H_D, 2, D_HEAD), gates


def retention_scan(q, k, v, log_g, state0, strict):
    B, n, H, dk = q.shape
    dv = v.shape[-1]
    nc = n // RET_CHUNK

    def chunks(t):
        return t.reshape(B, nc, RET_CHUNK, H, t.shape[-1]).transpose(1, 0, 3, 2, 4)

    pos = jnp.arange(RET_CHUNK, dtype=jnp.float32)
    dist = pos[:, None] - pos[None, :]
    keep = dist > 0 if strict else dist >= 0
    d_mask = jnp.where(keep, jnp.exp(log_g[:, None, None] * jnp.maximum(dist, 0.0)), 0.0)
    q_dec = jnp.exp(log_g[:, None] * (pos + 1.0))[:, :, None]
    k_dec = jnp.exp(log_g[:, None] * (RET_CHUNK - 1.0 - pos))[:, :, None]
    c_dec = jnp.exp(log_g * RET_CHUNK)[:, None, None]

    def step(S, blk):
        qb, kb, vb = blk
        scores = jnp.einsum('bhid,bhjd->bhij', qb, kb) * d_mask
        out = (jnp.einsum('bhij,bhje->bhie', scores, vb)
               + jnp.einsum('bhid,bhde->bhie', qb * q_dec, S))
        S = S * c_dec + jnp.einsum('bhjd,bhje->bhde', kb * k_dec, vb)
        return S, out

    _, outs = lax.scan(step, state0, (chunks(q), chunks(k), chunks(v)))
    return outs.transpose(1, 0, 3, 2, 4).reshape(B, n, H, dv)


def ret_ctx_states(rk, rv, log_g):
    L = rk.shape[1]
    m = jnp.arange(L, dtype=jnp.float32)
    k = rk.astype(jnp.float32)
    v = rv.astype(jnp.float32)
    w_f = jnp.exp(log_g[0][:, None] * (L - 1.0 - m))
    w_b = jnp.exp(log_g[1][:, None] * m)
    s_f = jnp.einsum('blhd,hl,blhe->bhde', k, w_f, v)
    s_b = jnp.einsum('blhd,hl,blhe->bhde', k, w_b, v)
    return s_f, s_b


def retention_branch(rq, rk, rv, rg, log_g, s_f, s_b, w_o):
    B, n = rq.shape[:2]
    f = jnp.float32
    q = rq.astype(f) * R_DK ** -0.5
    k = rk.astype(f)
    v = rv.astype(f)
    o_f = retention_scan(q, k, v, log_g[0], s_f, strict=False)
    o_b = retention_scan(q[:, ::-1], k[:, ::-1], v[:, ::-1], log_g[1], s_b, strict=True)[:, ::-1]
    o = o_f + o_b
    o = o * lax.rsqrt(jnp.mean(o * o, axis=-1, keepdims=True) + EPS)
    o = o.reshape(B, n, RV).astype(rg.dtype) * jax.nn.silu(rg)
    return o @ w_o


def diff_attend(q, k, v, lam):
    B, n, H, _, d = q.shape
    nb = n // Q_BLOCK
    qb = q.reshape(B, nb, Q_BLOCK, H, 2, d).transpose(1, 0, 2, 3, 4, 5)
    scale = d ** -0.5

    def one_block(qi):
        s = jnp.einsum('bqhcd,bkhcd->bchqk', qi, k) * scale
        p = jax.nn.softmax(s, axis=-1)
        a = p[:, 0] - lam * p[:, 1]
        return jnp.einsum('bhqk,bkhe->bqhe', a, v)

    o = lax.map(one_block, qb)
    return o.transpose(1, 0, 2, 3, 4).reshape(B, n, H, 2 * d)


def diff_branch(dq, dk, dv, lam, lam_init, subln_w, w_o):
    B, n = dq.shape[:2]
    f = jnp.float32
    o = diff_attend(dq.astype(f), dk.astype(f), dv.astype(f), lam)
    o = o * lax.rsqrt(jnp.mean(o * o, axis=-1, keepdims=True) + EPS) * subln_w.astype(f) * (1.0 - lam_init)
    return o.reshape(B, n, DVW).astype(dq.dtype) @ w_o


def mixer_output(rq, rg, dq, gates, rk, rv, k_att, v_att, s_f, s_b, log_g, lam, lam_init,
                 subln_w, w_ret_o, w_diff_o, w_out):
    ret_o = retention_branch(rq, rk, rv, rg, log_g, s_f, s_b, w_ret_o)
    diff_o = diff_branch(dq, k_att, v_att, lam, lam_init, subln_w, w_diff_o)
    g_r, g_d = jnp.split(gates, 2, axis=-1)
    return (jax.nn.sigmoid(g_r) * ret_o + jax.nn.sigmoid(g_d) * diff_o) @ w_out


def moe_ffn(h, w_router, router_bias, w_exp_gu, w_exp_down, w_sh_gu, w_sh_down):
    T = h.shape[0]
    f = jnp.float32
    scores = jax.nn.sigmoid(h.astype(f) @ w_router.astype(f).T)
    sel = scores + router_bias.astype(f)
    grp = sel.reshape(T, N_GROUPS, N_EXPERTS // N_GROUPS)
    grp_score = lax.top_k(grp, 2)[0].sum(-1)
    _, top_g = lax.top_k(grp_score, TOPK_GROUPS)
    gmask = jnp.any(top_g[..., None] == jnp.arange(N_GROUPS), axis=-2)
    emask = jnp.repeat(gmask, N_EXPERTS // N_GROUPS, axis=-1)
    _, eidx = lax.top_k(jnp.where(emask, sel, -jnp.inf), TOP_K)
    wts = jnp.take_along_axis(scores, eidx, axis=-1)
    wts = wts / jnp.sum(wts, axis=-1, keepdims=True) * ROUTED_SCALE

    n_assign = T * TOP_K
    e_flat = eidx.reshape(-1)
    t_flat = jnp.repeat(jnp.arange(T, dtype=jnp.int32), TOP_K)
    w_flat = wts.reshape(-1)
    order = jnp.argsort(e_flat)
    e_s, t_s, w_s = e_flat[order], t_flat[order], w_flat[order]
    counts = jax.ops.segment_sum(jnp.ones_like(e_flat), e_flat, num_segments=N_EXPERTS)
    padded = (counts + MOE_BLOCK - 1) // MOE_BLOCK * MOE_BLOCK
    starts = jnp.cumsum(counts) - counts
    p_ends = jnp.cumsum(padded)
    p_starts = p_ends - padded
    dest = p_starts[e_s] + jnp.arange(n_assign, dtype=jnp.int32) - starts[e_s]
    n_rows = -(-(n_assign + N_EXPERTS * (MOE_BLOCK - 1)) // MOE_BLOCK) * MOE_BLOCK
    row_tok = jnp.zeros((n_rows,), jnp.int32).at[dest].set(t_s)
    row_w = jnp.zeros((n_rows,), f).at[dest].set(w_s)
    n_blocks = n_rows // MOE_BLOCK
    blk_exp = jnp.minimum(
        jnp.searchsorted(p_ends, jnp.arange(n_blocks, dtype=jnp.int32) * MOE_BLOCK, side='right'),
        N_EXPERTS - 1)

    def step(acc, blk):
        rows, bw, e = blk
        g, u = jnp.split(h[rows] @ w_exp_gu[e], 2, axis=-1)
        y = (jax.nn.silu(g) * u) @ w_exp_down[e]
        return acc.at[rows].add(y * bw[:, None].astype(y.dtype)), None

    routed, _ = lax.scan(step, jnp.zeros_like(h),
                         (row_tok.reshape(n_blocks, MOE_BLOCK), row_w.reshape(n_blocks, MOE_BLOCK), blk_exp))
    gs, us = jnp.split(h @ w_sh_gu, 2, axis=-1)
    return routed + (jax.nn.silu(gs) * us) @ w_sh_down


def setup_inputs(seed: int = 0) -> dict:
    key = jax.random.key(seed)
    ks = jax.random.split(key, 24)
    f = jnp.float32
    D = D_MODEL

    def nrm(k, shape, scale=1.0):
        return jax.random.normal(k, shape, f) * scale

    decay0 = jnp.log(-jnp.log1p(-(2.0 ** (-5.0 - jnp.arange(H_R, dtype=f)))))
    return {
        'x': nrm(ks[0], (BATCH, SEQ, D)),
        'c': nrm(ks[1], (BATCH, D)),
        'ctx': nrm(ks[2], (BATCH, CTX_LEN, D)),
        'c_ctx': nrm(ks[3], (D,)),
        'w_mod': nrm(ks[4], (DEPTH, D, 6 * D), 0.5 * D ** -0.5),
        'b_mod': nrm(ks[5], (DEPTH, 6 * D), 0.01),
        'norm_attn': 1.0 + nrm(ks[6], (DEPTH, D), 0.05),
        'norm_ffn': 1.0 + nrm(ks[7], (DEPTH, D), 0.05),
        'w_in': nrm(ks[8], (DEPTH, D, IN_COLS), D ** -0.5),
        'ret_decay': decay0 + nrm(ks[9], (DEPTH, 2, H_R), 0.05),
        'diff_lambda': nrm(ks[10], (DEPTH, 4, D_HEAD), 0.1),
        'diff_subln': 1.0 + nrm(ks[11], (DEPTH, 2 * D_HEAD), 0.05),
        'w_ret_o': nrm(ks[12], (DEPTH, RV, D), RV ** -0.5),
        'w_diff_o': nrm(ks[13], (DEPTH, DVW, D), DVW ** -0.5),
        'w_out': nrm(ks[14], (DEPTH, D, D), D ** -0.5),
        'w_router': nrm(ks[15], (DEPTH, N_EXPERTS, D), D ** -0.5),
        'router_bias': nrm(ks[16], (DEPTH, N_EXPERTS), 0.01),
        'w_exp_gu': nrm(ks[17], (DEPTH, N_EXPERTS, D, 2 * D_EXPERT), D ** -0.5),
        'w_exp_down': nrm(ks[18], (DEPTH, N_EXPERTS, D_EXPERT, D), D_EXPERT ** -0.5),
        'w_sh_gu': nrm(ks[19], (DEPTH, D, 2 * D_SHARED), D ** -0.5),
        'w_sh_down': nrm(ks[20], (DEPTH, D_SHARED, D), D_SHARED ** -0.5),
        'final_norm': 1.0 + nrm(ks[21], (D,), 0.05),
    }


def reference(x, c, ctx, c_ctx, w_mod, b_mod, norm_attn, norm_ffn, w_in, ret_decay,
              diff_lambda, diff_subln, w_ret_o, w_diff_o, w_out, w_router, router_bias,
              w_exp_gu, w_exp_down, w_sh_gu, w_sh_down, final_norm):
    B, n_lat, D = x.shape
    rows = n_lat // GRID_W
    row = jnp.repeat(jnp.arange(rows, dtype=jnp.int32), GRID_W)
    col = jnp.tile(jnp.arange(GRID_W, dtype=jnp.int32), rows)
    xc = ctx
    silu_c = jax.nn.silu(c)
    silu_cc = jax.nn.silu(c_ctx)
    for l in range(DEPTH):
        last = l == DEPTH - 1
        lam_init = 0.8 - 0.6 * math.exp(-0.3 * l)
        mod = (silu_c @ w_mod[l] + b_mod[l])[:, None, :]
        mod_c = (silu_cc @ w_mod[l] + b_mod[l])[None, None, :]
        sh1, sc1, g1, sh2, sc2, g2 = jnp.split(mod, 6, axis=-1)
        sh1c, sc1c, g1c, sh2c, sc2c, g2c = jnp.split(mod_c, 6, axis=-1)
        log_g = -jnp.exp(ret_decay[l].astype(jnp.float32))
        dl = diff_lambda[l].astype(jnp.float32)
        lam = jnp.exp(jnp.sum(dl[0] * dl[1])) - jnp.exp(jnp.sum(dl[2] * dl[3])) + lam_init

        h = modulate(rms_norm(x, norm_attn[l]), sh1, sc1)
        hc = modulate(rms_norm(xc, norm_attn[l]), sh1c, sc1c)
        p = h @ w_in[l]
        pc = hc @ (w_in[l][:, :KV_COLS] if last else w_in[l])
        rq, rg, dq, gates = q_heads(p)
        rk, rv, dk, dv = kv_heads(p)
        rq, rk = axial_rope(rq, row, col), axial_rope(rk, row, col)
        dq, dk = axial_rope(dq, row, col), axial_rope(dk, row, col)
        rkc, rvc, dkc, dvc = kv_heads(pc)
        s_f, s_b = ret_ctx_states(rkc, rvc, log_g)
        mix = mixer_output(rq, rg, dq, gates, rk, rv,
                           jnp.concatenate([dkc, dk], axis=1), jnp.concatenate([dvc, dv], axis=1),
                           s_f, s_b, log_g, lam, lam_init, diff_subln[l], w_ret_o[l], w_diff_o[l], w_out[l])
        if not last:
            rqc, rgc, dqc, gatesc = q_heads(pc)
            zero_state = jnp.zeros_like(s_f)
            mix_c = mixer_output(rqc, rgc, dqc, gatesc, rkc, rvc, dkc, dvc, zero_state, zero_state,
                                 log_g, lam, lam_init, diff_subln[l], w_ret_o[l], w_diff_o[l], w_out[l])
            xc = xc + g1c * mix_c
            hc2 = modulate(rms_norm(xc, norm_ffn[l]), sh2c, sc2c)
            xc = xc + g2c * moe_ffn(hc2.reshape(-1, D), w_router[l], router_bias[l], w_exp_gu[l],
                                    w_exp_down[l], w_sh_gu[l], w_sh_down[l]).reshape(xc.shape)
        x = x + g1 * mix

        h2 = modulate(rms_norm(x, norm_ffn[l]), sh2, sc2)
        x = x + g2 * moe_ffn(h2.reshape(-1, D), w_router[l], router_bias[l], w_exp_gu[l],
                             w_exp_down[l], w_sh_gu[l], w_sh_down[l]).reshape(x.shape)
    return rms_norm(x, final_norm)
```

```python
import functools
import math

import jax
import jax.numpy as jnp
from jax import lax
from jax.experimental import pallas as pl
from jax.experimental.pallas import tpu as pltpu

F32 = jnp.float32
BF16 = jnp.bfloat16

D_MODEL = 2048
GRID_W = 64
EPS = 1e-6
ROPE_BASE = 10000.0
H_R = 8
R_DK = 128
R_DV = 256
RET_CHUNK = 128
H_D = 8
D_HEAD = 64
RK = H_R * R_DK
RV = H_R * R_DV
DKW = H_D * 2 * D_HEAD
DVW = H_D * 2 * D_HEAD
KV_COLS = RK + RV + DKW + DVW
N_EXPERTS = 256
TOP_K = 8
N_GROUPS = 8
TOPK_GROUPS = 4
GROUP_SIZE = N_EXPERTS // N_GROUPS
D_EXPERT = 256
D_SHARED = 256
ROUTED_SCALE = 2.5

VMEM_LIMIT_BYTES = 56 * 1024 * 1024
LANES = 128

MOE_BLOCK_ROWS = 256

_NT = (((1,), (1,)), ((), ()))
_TN = (((0,), (0,)), ((), ()))


def _params(sem):
    return pltpu.CompilerParams(dimension_semantics=sem, vmem_limit_bytes=VMEM_LIMIT_BYTES)


def _sigmoid(x):
    return 1.0 / (1.0 + jnp.exp(-x))


def _mod_kernel(s_ref, w_ref, b_ref, o_ref):
    acc = jnp.dot(s_ref[...].astype(BF16), w_ref[...].astype(BF16), preferred_element_type=F32)
    o_ref[...] = acc + b_ref[...]


def _mod_call(s, w, b, tn=1024):
    m, k = s.shape
    n = w.shape[1]
    return pl.pallas_call(
        _mod_kernel,
        grid=(n // tn,),
        in_specs=[pl.BlockSpec((m, k), lambda j: (0, 0)),
                  pl.BlockSpec((k, tn), lambda j: (0, j)),
                  pl.BlockSpec((1, tn), lambda j: (0, j))],
        out_specs=pl.BlockSpec((m, tn), lambda j: (0, j)),
        out_shape=jax.ShapeDtypeStruct((m, n), F32),
        compiler_params=_params(("arbitrary",)),
        name="mod_matvec",
    )(s, w, b)


def _norm_mod_kernel(x_ref, w_ref, sh_ref, o_ref):
    x = x_ref[...]
    ms = jnp.mean(x * x, axis=-1, keepdims=True)
    o_ref[...] = (x * lax.rsqrt(ms + EPS) * w_ref[0] + sh_ref[0]).astype(o_ref.dtype)


def _norm_mod_call(x2, w3, sh3, rows_per_batch, tm=512):
    t, d = x2.shape
    per = rows_per_batch // tm
    return pl.pallas_call(
        _norm_mod_kernel,
        grid=(t // tm,),
        in_specs=[pl.BlockSpec((tm, d), lambda i: (i, 0)),
                  pl.BlockSpec((1, 1, d), lambda i: (i // per, 0, 0)),
                  pl.BlockSpec((1, 1, d), lambda i: (i // per, 0, 0))],
        out_specs=pl.BlockSpec((tm, d), lambda i: (i, 0)),
        out_shape=jax.ShapeDtypeStruct((t, d), BF16),
        compiler_params=_params(("arbitrary",)),
        name="norm_mod",
    )(x2, w3, sh3)


def _mm_kernel(*refs, epilogue, n_extra):
    a_ref, w_ref = refs[0], refs[1]
    extra = refs[2:2 + n_extra]
    o_ref = refs[2 + n_extra]
    w_bf = refs[3 + n_extra]

    @pl.when(pl.program_id(1) == 0)
    def _():
        w_bf[...] = w_ref[...].astype(BF16)

    acc = jnp.dot(a_ref[...], w_bf[...], preferred_element_type=F32)
    epilogue(acc, o_ref, *extra)


def _mm_call(a, w, epilogue, extra, extra_specs, out_shape, out_spec, tm, tn, name, n_cols=None):
    t, k = a.shape
    n = w.shape[1] if n_cols is None else n_cols
    kern = functools.partial(_mm_kernel, epilogue=epilogue, n_extra=len(extra))
    return pl.pallas_call(
        kern,
        grid=(n // tn, t // tm),
        in_specs=[pl.BlockSpec((tm, k), lambda j, i: (i, 0)),
                  pl.BlockSpec((k, tn), lambda j, i: (0, j))] + list(extra_specs),
        out_specs=out_spec,
        out_shape=out_shape,
        scratch_shapes=[pltpu.VMEM((k, tn), BF16)],
        compiler_params=_params(("arbitrary", "arbitrary")),
        name=name,
    )(a, w, *extra)


def _plain_epilogue(acc, o_ref):
    o_ref[...] = acc.astype(o_ref.dtype)


_COL_RK, _COL_DK, _COL_RQ, _COL_DQ = 0, 3, 5, 8


def _rope_store(acc, o_ref, tab_ref, shift):
    c = tab_ref[0, 0]
    s_up = tab_ref[0, 1]
    s_dn = tab_ref[0, 2]
    for hh in range(acc.shape[1] // LANES):
        sl = slice(hh * LANES, (hh + 1) * LANES)
        xh = acc[:, sl]
        o = xh * c + pltpu.roll(xh, LANES - shift, 1) * s_up + pltpu.roll(xh, shift, 1) * s_dn
        o_ref[:, sl] = o.astype(o_ref.dtype)


def _proj_epilogue(acc, o_ref, tab_ref):
    j = pl.program_id(0)
    is_ret = jnp.logical_or(j == _COL_RK, j == _COL_RQ)
    is_diff = jnp.logical_or(j == _COL_DK, j == _COL_DQ)

    @pl.when(is_ret)
    def _():
        _rope_store(acc, o_ref, tab_ref, R_DK // 4)

    @pl.when(is_diff)
    def _():
        _rope_store(acc, o_ref, tab_ref, D_HEAD // 4)

    @pl.when(jnp.logical_not(jnp.logical_or(is_ret, is_diff)))
    def _():
        o_ref[...] = acc.astype(o_ref.dtype)


def _rope_tables(n_lat):
    t = jnp.arange(n_lat, dtype=jnp.int32)
    row = (t // GRID_W).astype(F32)
    col = (t % GRID_W).astype(F32)
    lane = jnp.arange(LANES, dtype=jnp.int32)

    def build(head_dim, scale):
        half_head = head_dim // 2
        half = half_head // 2
        l_in = lane % head_dim
        use_col = l_in >= half_head
        l_ax = l_in % half_head
        first = l_ax < half
        idx = (l_ax % half).astype(F32)
        inv = ROPE_BASE ** (-idx / half)
        pos = jnp.where(use_col[None, :], col[:, None], row[:, None])
        ang = pos * inv[None, :]
        cos = jnp.cos(ang)
        sin = jnp.sin(ang)
        s_up = jnp.where(first[None, :], -sin, 0.0)
        s_dn = jnp.where(first[None, :], 0.0, sin)
        return jnp.stack([cos, s_up, s_dn]) * scale

    return jnp.stack([build(R_DK, 1.0), build(R_DK, R_DK ** -0.5),
                      build(D_HEAD, 1.0), build(D_HEAD, D_HEAD ** -0.5)])


def _proj_call(h, w_in, tabs, n_lat, tm=1024, tn=1024):
    t = h.shape[0]
    n = w_in.shape[1]
    per = n_lat // tm

    def tab_idx(j, i):
        ty = jnp.where(j == _COL_RQ, 1, jnp.where(j == _COL_DK, 2, jnp.where(j == _COL_DQ, 3, 0)))
        return (ty, 0, i % per, 0)

    return _mm_call(
        h, w_in, _proj_epilogue, [tabs],
        [pl.BlockSpec((1, 3, tm, LANES), tab_idx)],
        jax.ShapeDtypeStruct((t, n), BF16),
        pl.BlockSpec((tm, tn), lambda j, i: (i, j)),
        tm, tn, "in_proj")


def _ret_kernel(logg_ref, q_ref, k_ref, v_ref, g_ref, kc_ref, vc_ref, o_ref, sb_ref):
    hh = pl.program_id(1)
    lgf = logg_ref[0, hh]
    lgb = logg_ref[1, hh]
    c_len = RET_CHUNK
    n = q_ref.shape[1]
    nc = n // c_len
    n_ctx = kc_ref.shape[1]

    ri = lax.broadcasted_iota(jnp.int32, (c_len, c_len), 0).astype(F32)
    ci = lax.broadcasted_iota(jnp.int32, (c_len, c_len), 1).astype(F32)
    dist = ri - ci
    dmat = jnp.where(dist >= 0.0,
                     jnp.exp(lgf * jnp.maximum(dist, 0.0)),
                     jnp.exp(lgb * jnp.maximum(-dist, 0.0)))
    rp = lax.broadcasted_iota(jnp.int32, (c_len, R_DK), 0).astype(F32)
    q_dec_f = jnp.exp(lgf * (rp + 1.0))
    q_dec_b = jnp.exp(lgb * (c_len - rp))
    k_dec_f = jnp.exp(lgf * (c_len - 1.0 - rp))
    k_dec_b = jnp.exp(lgb * rp)
    full = jnp.full((R_DK, R_DV), float(c_len), F32)
    c_dec_f = jnp.exp(lgf * full)
    c_dec_b = jnp.exp(lgb * full)

    mp = lax.broadcasted_iota(jnp.int32, (n_ctx, R_DK), 0).astype(F32)
    kc = kc_ref[0].astype(F32)
    vc = vc_ref[0]
    s_f0 = lax.dot_general((kc * jnp.exp(lgf * (n_ctx - 1.0 - mp))).astype(BF16), vc, _TN,
                           preferred_element_type=F32)
    s_b0 = lax.dot_general((kc * jnp.exp(lgb * mp)).astype(BF16), vc, _TN,
                           preferred_element_type=F32)

    def bwd_body(t, s):
        c = nc - 1 - t
        r0 = pl.multiple_of(c * c_len, c_len)
        sb_ref[c] = s.astype(BF16)
        kb = k_ref[0, pl.ds(r0, c_len), :].astype(F32)
        vb = v_ref[0, pl.ds(r0, c_len), :]
        upd = lax.dot_general((kb * k_dec_b).astype(BF16), vb, _TN, preferred_element_type=F32)
        return s * c_dec_b + upd

    lax.fori_loop(0, nc, bwd_body, s_b0)

    def fwd_body(c, s):
        r0 = pl.multiple_of(c * c_len, c_len)
        qb = q_ref[0, pl.ds(r0, c_len), :]
        kb = k_ref[0, pl.ds(r0, c_len), :]
        vb = v_ref[0, pl.ds(r0, c_len), :]
        sc = lax.dot_general(qb, kb, _NT, preferred_element_type=F32) * dmat
        o = jnp.dot(sc.astype(BF16), vb, preferred_element_type=F32)
        qf = qb.astype(F32)
        o = o + jnp.dot((qf * q_dec_f).astype(BF16), s.astype(BF16), preferred_element_type=F32)
        o = o + jnp.dot((qf * q_dec_b).astype(BF16), sb_ref[c], preferred_element_type=F32)
        o = o * lax.rsqrt(jnp.mean(o * o, axis=-1, keepdims=True) + EPS)
        g = g_ref[0, pl.ds(r0, c_len), :].astype(F32)
        o_ref[0, pl.ds(r0, c_len), :] = (o * (g * _sigmoid(g))).astype(o_ref.dtype)
        upd = lax.dot_general((kb.astype(F32) * k_dec_f).astype(BF16), vb, _TN,
                              preferred_element_type=F32)
        return s * c_dec_f + upd

    lax.fori_loop(0, nc, fwd_body, s_f0)


def _ret_call(log_g, p3, pc3):
    b, n, _ = p3.shape
    n_ctx = pc3.shape[1]
    nc = n // RET_CHUNK
    q_blk = (KV_COLS) // R_DK
    v_blk = RK // R_DV
    g_blk = (KV_COLS + RK) // R_DV
    return pl.pallas_call(
        _ret_kernel,
        grid=(b, H_R),
        in_specs=[pl.BlockSpec(memory_space=pltpu.SMEM),
                  pl.BlockSpec((1, n, R_DK), lambda bi, h: (bi, 0, q_blk + h)),
                  pl.BlockSpec((1, n, R_DK), lambda bi, h: (bi, 0, h)),
                  pl.BlockSpec((1, n, R_DV), lambda bi, h: (bi, 0, v_blk + h)),
                  pl.BlockSpec((1, n, R_DV), lambda bi, h: (bi, 0, g_blk + h)),
                  pl.BlockSpec((1, n_ctx, R_DK), lambda bi, h: (bi, 0, h)),
                  pl.BlockSpec((1, n_ctx, R_DV), lambda bi, h: (bi, 0, v_blk + h))],
        out_specs=pl.BlockSpec((1, n, R_DV), lambda bi, h: (bi, 0, h)),
        out_shape=jax.ShapeDtypeStruct((b, n, RV), BF16),
        scratch_shapes=[pltpu.VMEM((nc, R_DK, R_DV), BF16)],
        compiler_params=_params(("arbitrary", "arbitrary")),
        name="retention",
    )(log_g, p3, p3, p3, p3, pc3, pc3)


def _diff_kernel(lam_ref, q_ref, k_ref, v_ref, sub_ref, o_ref):
    q = q_ref[0].astype(F32)
    lane = lax.broadcasted_iota(jnp.int32, q.shape, 1)
    q0 = jnp.where(lane < D_HEAD, q, 0.0).astype(BF16)
    q1 = jnp.where(lane >= D_HEAD, q, 0.0).astype(BF16)
    k = k_ref[0]
    s0 = lax.dot_general(q0, k, _NT, preferred_element_type=F32)
    s1 = lax.dot_general(q1, k, _NT, preferred_element_type=F32)
    e0 = jnp.exp(s0 - jnp.max(s0, axis=-1, keepdims=True))
    e1 = jnp.exp(s1 - jnp.max(s1, axis=-1, keepdims=True))
    l0 = jnp.sum(e0, axis=-1, keepdims=True)
    l1 = jnp.sum(e1, axis=-1, keepdims=True)
    a = e0 * (1.0 / l0) - e1 * (lam_ref[0, 0] / l1)
    o = jnp.dot(a.astype(BF16), v_ref[0], preferred_element_type=F32)
    o = o * lax.rsqrt(jnp.mean(o * o, axis=-1, keepdims=True) + EPS) * sub_ref[...]
    o_ref[0] = o.astype(o_ref.dtype)


def _diff_call(lam, p3, k_att, v_att, sub, tq=256):
    b, n, _ = p3.shape
    m = k_att.shape[1]
    hd = 2 * D_HEAD
    q_blk = (KV_COLS + RK + RV) // hd
    return pl.pallas_call(
        _diff_kernel,
        grid=(b, H_D, n // tq),
        in_specs=[pl.BlockSpec(memory_space=pltpu.SMEM),
                  pl.BlockSpec((1, tq, hd), lambda bi, h, qi: (bi, qi, q_blk + h)),
                  pl.BlockSpec((1, m, hd), lambda bi, h, qi: (bi, 0, h)),
                  pl.BlockSpec((1, m, hd), lambda bi, h, qi: (bi, 0, h)),
                  pl.BlockSpec((1, hd), lambda bi, h, qi: (0, 0))],
        out_specs=pl.BlockSpec((1, tq, hd), lambda bi, h, qi: (bi, qi, h)),
        out_shape=jax.ShapeDtypeStruct((b, n, DVW), BF16),
        compiler_params=_params(("arbitrary", "arbitrary", "arbitrary")),
        name="diff_attn",
    )(lam, p3, k_att, v_att, sub)


def _gate_epilogue(acc, o_ref, gate_ref):
    o_ref[...] = (_sigmoid(gate_ref[...].astype(F32)) * acc).astype(o_ref.dtype)


def _gate_add_epilogue(acc, o_ref, gate_ref, prev_ref):
    o = prev_ref[...].astype(F32) + _sigmoid(gate_ref[...].astype(F32)) * acc
    o_ref[...] = o.astype(o_ref.dtype)


def _resid_epilogue(acc, o_ref, x_ref, g_ref):
    o_ref[...] = x_ref[...] + g_ref[0] * acc


def _swiglu_epilogue(acc, o_ref):
    half = acc.shape[1] // 2
    g = acc[:, :half]
    u = acc[:, half:]
    o_ref[...] = (g * _sigmoid(g) * u).astype(o_ref.dtype)


def _final_epilogue(acc, o_ref, routed_ref, x_ref, g_ref, fn_ref):
    y = x_ref[...] + g_ref[0] * (routed_ref[...] + acc)
    ms = jnp.mean(y * y, axis=-1, keepdims=True)
    o_ref[...] = y * lax.rsqrt(ms + EPS) * fn_ref[...]


def _router_kernel(x_ref, w_ref, sh_ref, wr_ref, bias_ref, h_ref, ei_ref, wt_ref):
    x = x_ref[...]
    ms = jnp.mean(x * x, axis=-1, keepdims=True)
    y = x * lax.rsqrt(ms + EPS) * w_ref[0] + sh_ref[0]
    h_ref[...] = y.astype(h_ref.dtype)
    logits = lax.dot_general(wr_ref[...], y, _NT, precision=lax.Precision.HIGHEST,
                             preferred_element_type=F32)
    scores = _sigmoid(logits)
    sel = scores + bias_ref[...]
    tm = sel.shape[1]
    neg = float("-inf")

    io_g = lax.broadcasted_iota(jnp.int32, (GROUP_SIZE, tm), 0).astype(F32)
    gscore = []
    for g in range(N_GROUPS):
        v = sel[g * GROUP_SIZE:(g + 1) * GROUP_SIZE, :]
        m1 = jnp.max(v, axis=0, keepdims=True)
        i1 = jnp.min(jnp.where(v == m1, io_g, float(GROUP_SIZE)), axis=0, keepdims=True)
        m2 = jnp.max(jnp.where(io_g == i1, neg, v), axis=0, keepdims=True)
        gscore.append(m1 + m2)

    parts = []
    for g in range(N_GROUPS):
        cnt = jnp.zeros((1, tm), F32)
        for g2 in range(N_GROUPS):
            if g2 == g:
                continue
            beats = (gscore[g2] >= gscore[g]) if g2 < g else (gscore[g2] > gscore[g])
            cnt = cnt + jnp.where(beats, 1.0, 0.0)
        keep = cnt < float(TOPK_GROUPS)
        parts.append(jnp.where(keep, sel[g * GROUP_SIZE:(g + 1) * GROUP_SIZE, :], neg))
    vm = jnp.concatenate(parts, axis=0)

    io = lax.broadcasted_iota(jnp.int32, (N_EXPERTS, tm), 0).astype(F32)
    idxs, ws = [], []
    for _ in range(TOP_K):
        m = jnp.max(vm, axis=0, keepdims=True)
        i = jnp.min(jnp.where(vm == m, io, float(N_EXPERTS)), axis=0, keepdims=True)
        hit = io == i
        ws.append(jnp.sum(jnp.where(hit, scores, 0.0), axis=0, keepdims=True))
        idxs.append(i)
        vm = jnp.where(hit, neg, vm)
    w = jnp.concatenate(ws, axis=0)
    w = w / jnp.sum(w, axis=0, keepdims=True) * ROUTED_SCALE
    ei_ref[...] = jnp.concatenate(idxs, axis=0).astype(jnp.int32)
    wt_ref[...] = w


def _router_call(x2, w3, sh3, w_router, bias_b, rows_per_batch, tm):
    t, d = x2.shape
    per = rows_per_batch // tm
    return pl.pallas_call(
        _router_kernel,
        grid=(t // tm,),
        in_specs=[pl.BlockSpec((tm, d), lambda i: (i, 0)),
                  pl.BlockSpec((1, 1, d), lambda i: (i // per, 0, 0)),
                  pl.BlockSpec((1, 1, d), lambda i: (i // per, 0, 0)),
                  pl.BlockSpec((N_EXPERTS, d), lambda i: (0, 0)),
                  pl.BlockSpec((N_EXPERTS, tm), lambda i: (0, 0))],
        out_specs=[pl.BlockSpec((tm, d), lambda i: (i, 0)),
                   pl.BlockSpec((TOP_K, tm), lambda i: (0, i)),
                   pl.BlockSpec((TOP_K, tm), lambda i: (0, i))],
        out_shape=[jax.ShapeDtypeStruct((t, d), BF16),
                   jax.ShapeDtypeStruct((TOP_K, t), jnp.int32),
                   jax.ShapeDtypeStruct((TOP_K, t), F32)],
        compiler_params=_params(("arbitrary",)),
        name="norm_router",
    )(x2, w3, sh3, w_router, bias_b)


def _moe_kernel(be_ref, nu_ref, x_ref, wgu_ref, wd_ref, o_ref, wgu_bf, wd_bf):
    b = pl.program_id(0)

    @pl.when(b < nu_ref[0])
    def _():
        prev = be_ref[jnp.maximum(b - 1, 0)]

        @pl.when(jnp.logical_or(b == 0, be_ref[b] != prev))
        def _():
            wgu_bf[...] = wgu_ref[0].astype(BF16)
            wd_bf[...] = wd_ref[0].astype(BF16)

        gu = jnp.dot(x_ref[...], wgu_bf[...], preferred_element_type=F32)
        g = gu[:, :D_EXPERT]
        u = gu[:, D_EXPERT:]
        act = (g * _sigmoid(g) * u).astype(BF16)
        o_ref[...] = jnp.dot(act, wd_bf[...], preferred_element_type=F32).astype(o_ref.dtype)


def _moe_call(blk_exp, n_used, x_sorted, w_gu, w_down, bm):
    n_rows, d = x_sorted.shape
    n_blocks = n_rows // bm

    def row_idx(b, be, nu):
        return (jnp.minimum(b, nu[0] - 1), 0)

    grid_spec = pltpu.PrefetchScalarGridSpec(
        num_scalar_prefetch=2,
        grid=(n_blocks,),
        in_specs=[pl.BlockSpec((bm, d), row_idx),
                  pl.BlockSpec((1, d, 2 * D_EXPERT), lambda b, be, nu: (be[b], 0, 0)),
                  pl.BlockSpec((1, D_EXPERT, d), lambda b, be, nu: (be[b], 0, 0))],
        out_specs=pl.BlockSpec((bm, d), row_idx),
        scratch_shapes=[pltpu.VMEM((d, 2 * D_EXPERT), BF16),
                        pltpu.VMEM((D_EXPERT, d), BF16)],
    )
    return pl.pallas_call(
        _moe_kernel,
        grid_spec=grid_spec,
        out_shape=jax.ShapeDtypeStruct((n_rows, d), BF16),
        compiler_params=_params(("arbitrary",)),
        name="moe_experts",
    )(blk_exp, n_used, x_sorted, w_gu, w_down)


def _dispatch(eidx, bm):
    k, t = eidx.shape
    n_assign = k * t
    e_flat = eidx.reshape(-1)
    t_flat = jnp.tile(jnp.arange(t, dtype=jnp.int32), k)
    order = jnp.argsort(e_flat)
    e_s = e_flat[order]
    counts = jnp.zeros((N_EXPERTS,), jnp.int32).at[e_flat].add(1)
    padded = (counts + bm - 1) // bm * bm
    starts = jnp.cumsum(counts) - counts
    p_ends = jnp.cumsum(padded)
    p_starts = p_ends - padded
    dest = p_starts[e_s] + jnp.arange(n_assign, dtype=jnp.int32) - starts[e_s]
    n_rows = -(-(n_assign + N_EXPERTS * (bm - 1)) // bm) * bm
    row_tok = jnp.zeros((n_rows,), jnp.int32).at[dest].set(t_flat[order])
    pos = jnp.zeros((n_assign,), jnp.int32).at[order].set(dest)
    n_blocks = n_rows // bm
    blk_exp = jnp.minimum(
        jnp.searchsorted(p_ends, jnp.arange(n_blocks, dtype=jnp.int32) * bm, side="right"),
        N_EXPERTS - 1).astype(jnp.int32)
    n_used = (p_ends[-1] // bm).astype(jnp.int32).reshape(1)
    return row_tok, pos.reshape(k, t), blk_exp, n_used


def kernel(x, c, ctx, c_ctx, w_mod, b_mod, norm_attn, norm_ffn, w_in, ret_decay, diff_lambda,
           diff_subln, w_ret_o, w_diff_o, w_out, w_router, router_bias, w_exp_gu, w_exp_down,
           w_sh_gu, w_sh_down, final_norm):
    bsz, n_lat, d = x.shape
    n_ctx = ctx.shape[1]
    t = bsz * n_lat
    assert w_mod.shape[0] == 1 and d == D_MODEL
    lam_init = 0.8 - 0.6 * math.exp(-0.3 * 0)

    s = jnp.concatenate([jax.nn.silu(c), jax.nn.silu(c_ctx)[None, :],
                         jnp.zeros((8 - bsz - 1, d), F32)], axis=0)
    mod = _mod_call(s, w_mod[0], b_mod[0][None, :])
    sh1, sc1, g1, sh2, sc2, g2 = [mod[:, i * d:(i + 1) * d] for i in range(6)]
    w1 = (norm_attn[0][None, :] * (1.0 + sc1))[:, None, :]
    w2 = (norm_ffn[0][None, :] * (1.0 + sc2))[:, None, :]
    sh1 = sh1[:, None, :]
    sh2 = sh2[:, None, :]
    g1 = g1[:, None, :]
    g2 = g2[:, None, :]

    log_g = -jnp.exp(ret_decay[0].astype(F32))
    dl = diff_lambda[0].astype(F32)
    lam = (jnp.exp(jnp.sum(dl[0] * dl[1])) - jnp.exp(jnp.sum(dl[2] * dl[3])) + lam_init).reshape(1, 1)
    sub = (diff_subln[0].astype(F32) * (1.0 - lam_init))[None, :]

    x2 = x.reshape(t, d)
    h = _norm_mod_call(x2, w1[:bsz], sh1[:bsz], n_lat)
    hc = _norm_mod_call(ctx.reshape(bsz * n_ctx, d), w1[bsz:bsz + 1], sh1[bsz:bsz + 1], bsz * n_ctx,
                        tm=bsz * n_ctx)
    tabs = _rope_tables(n_lat)
    p = _proj_call(h, w_in[0], tabs, n_lat)
    tc = bsz * n_ctx
    pc = _mm_call(hc, w_in[0], _plain_epilogue, [], [],
                  jax.ShapeDtypeStruct((tc, KV_COLS), BF16),
                  pl.BlockSpec((tc, 1024), lambda j, i: (i, j)),
                  tc, 1024, "ctx_proj", n_cols=KV_COLS)
    p3 = p.reshape(bsz, n_lat, p.shape[1])
    pc3 = pc.reshape(bsz, n_ctx, KV_COLS)

    og = _ret_call(log_g, p3, pc3)

    dk0 = RK + RV
    k_att = jnp.concatenate([pc3[:, :, dk0:dk0 + DKW], p3[:, :, dk0:dk0 + DKW]], axis=1)
    v_att = jnp.concatenate([pc3[:, :, dk0 + DKW:KV_COLS], p3[:, :, dk0 + DKW:KV_COLS]], axis=1)
    od = _diff_call(lam, p3, k_att, v_att, sub)

    tm, tn = 1024, 1024
    per = n_lat // tm
    gate_blk = (KV_COLS + RK + RV + DKW) // tn
    r1 = _mm_call(og.reshape(t, RV), w_ret_o[0], _gate_epilogue, [p],
                  [pl.BlockSpec((tm, tn), lambda j, i: (i, gate_blk + j))],
                  jax.ShapeDtypeStruct((t, d), BF16),
                  pl.BlockSpec((tm, tn), lambda j, i: (i, j)), tm, tn, "ret_out")
    merged = _mm_call(od.reshape(t, DVW), w_diff_o[0], _gate_add_epilogue, [p, r1],
                      [pl.BlockSpec((tm, tn), lambda j, i: (i, gate_blk + d // tn + j)),
                       pl.BlockSpec((tm, tn), lambda j, i: (i, j))],
                      jax.ShapeDtypeStruct((t, d), BF16),
                      pl.BlockSpec((tm, tn), lambda j, i: (i, j)), tm, tn, "diff_out")
    x1 = _mm_call(merged, w_out[0], _resid_epilogue, [x2, g1[:bsz]],
                  [pl.BlockSpec((tm, tn), lambda j, i: (i, j)),
                   pl.BlockSpec((1, 1, tn), lambda j, i: (i // per, 0, j))],
                  jax.ShapeDtypeStruct((t, d), F32),
                  pl.BlockSpec((tm, tn), lambda j, i: (i, j)), tm, tn, "mix_out")

    rt = 512
    bias_b = jnp.broadcast_to(router_bias[0].astype(F32)[:, None], (N_EXPERTS, rt))
    h2, eidx, wts = _router_call(x1, w2[:bsz], sh2[:bsz], w_router[0], bias_b, n_lat, rt)

    bm = MOE_BLOCK_ROWS
    row_tok, pos, blk_exp, n_used = _dispatch(eidx, bm)
    x_sorted = h2[row_tok]
    y_sorted = _moe_call(blk_exp, n_used, x_sorted, w_exp_gu[0], w_exp_down[0], bm)
    routed = jnp.sum(y_sorted[pos].astype(F32) * wts[:, :, None], axis=0)

    sh_act = _mm_call(h2, w_sh_gu[0], _swiglu_epilogue, [], [],
                      jax.ShapeDtypeStruct((t, D_SHARED), BF16),
                      pl.BlockSpec((tm, D_SHARED), lambda j, i: (i, j)),
                      tm, 2 * D_SHARED, "shared_up")
    fm = 512
    per_f = n_lat // fm
    out = _mm_call(sh_act, w_sh_down[0], _final_epilogue,
                   [routed, x1, g2[:bsz], final_norm[None, :]],
                   [pl.BlockSpec((fm, d), lambda j, i: (i, 0)),
                    pl.BlockSpec((fm, d), lambda j, i: (i, 0)),
                    pl.BlockSpec((1, 1, d), lambda j, i: (i // per_f, 0, 0)),
                    pl.BlockSpec((1, d), lambda j, i: (0, 0))],
                   jax.ShapeDtypeStruct((t, d), F32),
                   pl.BlockSpec((fm, d), lambda j, i: (i, 0)), fm, d, "shared_down_final")
    return out.reshape(bsz, n_lat, d)
```

```python
import functools
import math

import jax
import jax.numpy as jnp
from jax import lax
from jax.experimental import pallas as pl
from jax.experimental.pallas import tpu as pltpu

F32 = jnp.float32
BF16 = jnp.bfloat16

D_MODEL = 2048
GRID_W = 64
EPS = 1e-6
ROPE_BASE = 10000.0
H_R = 8
R_DK = 128
R_DV = 256
RET_CHUNK = 128
H_D = 8
D_HEAD = 64
RK = H_R * R_DK
RV = H_R * R_DV
DKW = H_D * 2 * D_HEAD
DVW = H_D * 2 * D_HEAD
KV_COLS = RK + RV + DKW + DVW
N_EXPERTS = 256
TOP_K = 8
N_GROUPS = 8
TOPK_GROUPS = 4
GROUP_SIZE = N_EXPERTS // N_GROUPS
D_EXPERT = 256
D_SHARED = 256
ROUTED_SCALE = 2.5

VMEM_LIMIT_BYTES = 56 * 1024 * 1024
LANES = 128

MOE_BLOCK_ROWS = 256

_NT = (((1,), (1,)), ((), ()))
_TN = (((0,), (0,)), ((), ()))


def _params(sem):
    return pltpu.CompilerParams(dimension_semantics=sem, vmem_limit_bytes=VMEM_LIMIT_BYTES)


def _sigmoid(x):
    return 1.0 / (1.0 + jnp.exp(-x))


def _mod_kernel(s_ref, w_ref, b_ref, o_ref):
    acc = jnp.dot(s_ref[...].astype(BF16), w_ref[...].astype(BF16), preferred_element_type=F32)
    o_ref[...] = acc + b_ref[...]


def _mod_call(s, w, b, tn=1024):
    m, k = s.shape
    n = w.shape[1]
    return pl.pallas_call(
        _mod_kernel,
        grid=(n // tn,),
        in_specs=[pl.BlockSpec((m, k), lambda j: (0, 0)),
                  pl.BlockSpec((k, tn), lambda j: (0, j)),
                  pl.BlockSpec((1, tn), lambda j: (0, j))],
        out_specs=pl.BlockSpec((m, tn), lambda j: (0, j)),
        out_shape=jax.ShapeDtypeStruct((m, n), F32),
        compiler_params=_params(("arbitrary",)),
        name="mod_matvec",
    )(s, w, b)


def _norm_mod_kernel(x_ref, w_ref, sh_ref, o_ref):
    x = x_ref[...]
    ms = jnp.mean(x * x, axis=-1, keepdims=True)
    o_ref[...] = (x * lax.rsqrt(ms + EPS) * w_ref[0] + sh_ref[0]).astype(o_ref.dtype)


def _norm_mod_call(x2, w3, sh3, rows_per_batch, tm=512):
    t, d = x2.shape
    per = rows_per_batch // tm
    return pl.pallas_call(
        _norm_mod_kernel,
        grid=(t // tm,),
        in_specs=[pl.BlockSpec((tm, d), lambda i: (i, 0)),
                  pl.BlockSpec((1, 1, d), lambda i: (i // per, 0, 0)),
                  pl.BlockSpec((1, 1, d), lambda i: (i // per, 0, 0))],
        out_specs=pl.BlockSpec((tm, d), lambda i: (i, 0)),
        out_shape=jax.ShapeDtypeStruct((t, d), BF16),
        compiler_params=_params(("arbitrary",)),
        name="norm_mod",
    )(x2, w3, sh3)


def _mm_kernel(*refs, epilogue, n_extra):
    a_ref, w_ref = refs[0], refs[1]
    extra = refs[2:2 + n_extra]
    o_ref = refs[2 + n_extra]
    w_bf = refs[3 + n_extra]

    @pl.when(pl.program_id(1) == 0)
    def _():
        w_bf[...] = w_ref[...].astype(BF16)

    acc = jnp.dot(a_ref[...], w_bf[...], preferred_element_type=F32)
    epilogue(acc, o_ref, *extra)


def _mm_call(a, w, epilogue, extra, extra_specs, out_shape, out_spec, tm, tn, name, n_cols=None):
    t, k = a.shape
    n = w.shape[1] if n_cols is None else n_cols
    kern = functools.partial(_mm_kernel, epilogue=epilogue, n_extra=len(extra))
    return pl.pallas_call(
        kern,
        grid=(n // tn, t // tm),
        in_specs=[pl.BlockSpec((tm, k), lambda j, i: (i, 0)),
                  pl.BlockSpec((k, tn), lambda j, i: (0, j))] + list(extra_specs),
        out_specs=out_spec,
        out_shape=out_shape,
        scratch_shapes=[pltpu.VMEM((k, tn), BF16)],
        compiler_params=_params(("arbitrary", "arbitrary")),
        name=name,
    )(a, w, *extra)


def _plain_epilogue(acc, o_ref):
    o_ref[...] = acc.astype(o_ref.dtype)


_COL_RK, _COL_DK, _COL_RQ, _COL_DQ = 0, 3, 5, 8


def _rope_store(acc, o_ref, tab_ref, shift):
    c = tab_ref[0, 0]
    s_up = tab_ref[0, 1]
    s_dn = tab_ref[0, 2]
    for hh in range(acc.shape[1] // LANES):
        sl = slice(hh * LANES, (hh + 1) * LANES)
        xh = acc[:, sl]
        o = xh * c + pltpu.roll(xh, LANES - shift, 1) * s_up + pltpu.roll(xh, shift, 1) * s_dn
        o_ref[:, sl] = o.astype(o_ref.dtype)


def _proj_epilogue(acc, o_ref, tab_ref):
    j = pl.program_id(0)
    is_ret = jnp.logical_or(j == _COL_RK, j == _COL_RQ)
    is_diff = jnp.logical_or(j == _COL_DK, j == _COL_DQ)

    @pl.when(is_ret)
    def _():
        _rope_store(acc, o_ref, tab_ref, R_DK // 4)

    @pl.when(is_diff)
    def _():
        _rope_store(acc, o_ref, tab_ref, D_HEAD // 4)

    @pl.when(jnp.logical_not(jnp.logical_or(is_ret, is_diff)))
    def _():
        o_ref[...] = acc.astype(o_ref.dtype)


def _rope_tables(n_lat):
    t = jnp.arange(n_lat, dtype=jnp.int32)
    row = (t // GRID_W).astype(F32)
    col = (t % GRID_W).astype(F32)
    lane = jnp.arange(LANES, dtype=jnp.int32)

    def build(head_dim, scale):
        half_head = head_dim // 2
        half = half_head // 2
        l_in = lane % head_dim
        use_col = l_in >= half_head
        l_ax = l_in % half_head
        first = l_ax < half
        idx = (l_ax % half).astype(F32)
        inv = ROPE_BASE ** (-idx / half)
        pos = jnp.where(use_col[None, :], col[:, None], row[:, None])
        ang = pos * inv[None, :]
        cos = jnp.cos(ang)
        sin = jnp.sin(ang)
        s_up = jnp.where(first[None, :], -sin, 0.0)
        s_dn = jnp.where(first[None, :], 0.0, sin)
        return jnp.stack([cos, s_up, s_dn]) * scale

    return jnp.stack([build(R_DK, 1.0), build(R_DK, R_DK ** -0.5),
                      build(D_HEAD, 1.0), build(D_HEAD, D_HEAD ** -0.5)])


def _proj_call(h, w_in, tabs, n_lat, tm=1024, tn=1024):
    t = h.shape[0]
    n = w_in.shape[1]
    per = n_lat // tm

    def tab_idx(j, i):
        ty = jnp.where(j == _COL_RQ, 1, jnp.where(j == _COL_DK, 2, jnp.where(j == _COL_DQ, 3, 0)))
        return (ty, 0, i % per, 0)

    return _mm_call(
        h, w_in, _proj_epilogue, [tabs],
        [pl.BlockSpec((1, 3, tm, LANES), tab_idx)],
        jax.ShapeDtypeStruct((t, n), BF16),
        pl.BlockSpec((tm, tn), lambda j, i: (i, j)),
        tm, tn, "in_proj")


def _ret_kernel(logg_ref, q_ref, k_ref, v_ref, g_ref, kc_ref, vc_ref, o_ref, sb_ref):
    hh = pl.program_id(1)
    lgf = logg_ref[0, hh]
    lgb = logg_ref[1, hh]
    c_len = RET_CHUNK
    n = q_ref.shape[1]
    nc = n // c_len
    n_ctx = kc_ref.shape[1]

    ri = lax.broadcasted_iota(jnp.int32, (c_len, c_len), 0).astype(F32)
    ci = lax.broadcasted_iota(jnp.int32, (c_len, c_len), 1).astype(F32)
    dist = ri - ci
    dmat = jnp.where(dist >= 0.0,
                     jnp.exp(lgf * jnp.maximum(dist, 0.0)),
                     jnp.exp(lgb * jnp.maximum(-dist, 0.0)))
    rp = lax.broadcasted_iota(jnp.int32, (c_len, R_DK), 0).astype(F32)
    q_dec_f = jnp.exp(lgf * (rp + 1.0))
    q_dec_b = jnp.exp(lgb * (c_len - rp))
    k_dec_f = jnp.exp(lgf * (c_len - 1.0 - rp))
    k_dec_b = jnp.exp(lgb * rp)
    full = jnp.full((R_DK, R_DV), float(c_len), F32)
    c_dec_f = jnp.exp(lgf * full)
    c_dec_b = jnp.exp(lgb * full)

    mp = lax.broadcasted_iota(jnp.int32, (n_ctx, R_DK), 0).astype(F32)
    kc = kc_ref[0].astype(F32)
    vc = vc_ref[0]
    s_f0 = lax.dot_general((kc * jnp.exp(lgf * (n_ctx - 1.0 - mp))).astype(BF16), vc, _TN,
                           preferred_element_type=F32)
    s_b0 = lax.dot_general((kc * jnp.exp(lgb * mp)).astype(BF16), vc, _TN,
                           preferred_element_type=F32)

    def bwd_body(t, s):
        c = nc - 1 - t
        r0 = pl.multiple_of(c * c_len, c_len)
        sb_ref[c] = s.astype(BF16)
        kb = k_ref[0, pl.ds(r0, c_len), :].astype(F32)
        vb = v_ref[0, pl.ds(r0, c_len), :]
        upd = lax.dot_general((kb * k_dec_b).astype(BF16), vb, _TN, preferred_element_type=F32)
        return s * c_dec_b + upd

    lax.fori_loop(0, nc, bwd_body, s_b0)

    def fwd_body(c, s):
        r0 = pl.multiple_of(c * c_len, c_len)
        qb = q_ref[0, pl.ds(r0, c_len), :]
        kb = k_ref[0, pl.ds(r0, c_len), :]
        vb = v_ref[0, pl.ds(r0, c_len), :]
        sc = lax.dot_general(qb, kb, _NT, preferred_element_type=F32) * dmat
        o = jnp.dot(sc.astype(BF16), vb, preferred_element_type=F32)
        qf = qb.astype(F32)
        o = o + jnp.dot((qf * q_dec_f).astype(BF16), s.astype(BF16), preferred_element_type=F32)
        o = o + jnp.dot((qf * q_dec_b).astype(BF16), sb_ref[c], preferred_element_type=F32)
        o = o * lax.rsqrt(jnp.mean(o * o, axis=-1, keepdims=True) + EPS)
        g = g_ref[0, pl.ds(r0, c_len), :].astype(F32)
        o_ref[0, pl.ds(r0, c_len), :] = (o * (g * _sigmoid(g))).astype(o_ref.dtype)
        upd = lax.dot_general((kb.astype(F32) * k_dec_f).astype(BF16), vb, _TN,
                              preferred_element_type=F32)
        return s * c_dec_f + upd

    lax.fori_loop(0, nc, fwd_body, s_f0)


def _ret_call(log_g, p3, pc3):
    b, n, _ = p3.shape
    n_ctx = pc3.shape[1]
    nc = n // RET_CHUNK
    q_blk = (KV_COLS) // R_DK
    v_blk = RK // R_DV
    g_blk = (KV_COLS + RK) // R_DV
    return pl.pallas_call(
        _ret_kernel,
        grid=(b, H_R),
        in_specs=[pl.BlockSpec(memory_space=pltpu.SMEM),
                  pl.BlockSpec((1, n, R_DK), lambda bi, h: (bi, 0, q_blk + h)),
                  pl.BlockSpec((1, n, R_DK), lambda bi, h: (bi, 0, h)),
                  pl.BlockSpec((1, n, R_DV), lambda bi, h: (bi, 0, v_blk + h)),
                  pl.BlockSpec((1, n, R_DV), lambda bi, h: (bi, 0, g_blk + h)),
                  pl.BlockSpec((1, n_ctx, R_DK), lambda bi, h: (bi, 0, h)),
                  pl.BlockSpec((1, n_ctx, R_DV), lambda bi, h: (bi, 0, v_blk + h))],
        out_specs=pl.BlockSpec((1, n, R_DV), lambda bi, h: (bi, 0, h)),
        out_shape=jax.ShapeDtypeStruct((b, n, RV), BF16),
        scratch_shapes=[pltpu.VMEM((nc, R_DK, R_DV), BF16)],
        compiler_params=_params(("arbitrary", "arbitrary")),
        name="retention",
    )(log_g, p3, p3, p3, p3, pc3, pc3)


def _diff_kernel(lam_ref, q_ref, k_ref, v_ref, sub_ref, o_ref):
    q = q_ref[0].astype(F32)
    lane = lax.broadcasted_iota(jnp.int32, q.shape, 1)
    q0 = jnp.where(lane < D_HEAD, q, 0.0).astype(BF16)
    q1 = jnp.where(lane >= D_HEAD, q, 0.0).astype(BF16)
    k = k_ref[0]
    s0 = lax.dot_general(q0, k, _NT, preferred_element_type=F32)
    s1 = lax.dot_general(q1, k, _NT, preferred_element_type=F32)
    e0 = jnp.exp(s0 - jnp.max(s0, axis=-1, keepdims=True))
    e1 = jnp.exp(s1 - jnp.max(s1, axis=-1, keepdims=True))
    l0 = jnp.sum(e0, axis=-1, keepdims=True)
    l1 = jnp.sum(e1, axis=-1, keepdims=True)
    a = e0 * (1.0 / l0) - e1 * (lam_ref[0, 0] / l1)
    o = jnp.dot(a.astype(BF16), v_ref[0], preferred_element_type=F32)
    o = o * lax.rsqrt(jnp.mean(o * o, axis=-1, keepdims=True) + EPS) * sub_ref[...]
    o_ref[0] = o.astype(o_ref.dtype)


def _diff_call(lam, p3, k_att, v_att, sub, tq=256):
    b, n, _ = p3.shape
    m = k_att.shape[1]
    hd = 2 * D_HEAD
    q_blk = (KV_COLS + RK + RV) // hd
    return pl.pallas_call(
        _diff_kernel,
        grid=(b, H_D, n // tq),
        in_specs=[pl.BlockSpec(memory_space=pltpu.SMEM),
                  pl.BlockSpec((1, tq, hd), lambda bi, h, qi: (bi, qi, q_blk + h)),
                  pl.BlockSpec((1, m, hd), lambda bi, h, qi: (bi, 0, h)),
                  pl.BlockSpec((1, m, hd), lambda bi, h, qi: (bi, 0, h)),
                  pl.BlockSpec((1, hd), lambda bi, h, qi: (0, 0))],
        out_specs=pl.BlockSpec((1, tq, hd), lambda bi, h, qi: (bi, qi, h)),
        out_shape=jax.ShapeDtypeStruct((b, n, DVW), BF16),
        compiler_params=_params(("arbitrary", "arbitrary", "arbitrary")),
        name="diff_attn",
    )(lam, p3, k_att, v_att, sub)


def _gate_epilogue(acc, o_ref, gate_ref):
    o_ref[...] = (_sigmoid(gate_ref[...].astype(F32)) * acc).astype(o_ref.dtype)


def _gate_add_epilogue(acc, o_ref, gate_ref, prev_ref):
    o = prev_ref[...].astype(F32) + _sigmoid(gate_ref[...].astype(F32)) * acc
    o_ref[...] = o.astype(o_ref.dtype)


def _resid_epilogue(acc, o_ref, x_ref, g_ref):
    o_ref[...] = x_ref[...] + g_ref[0] * acc


def _swiglu_epilogue(acc, o_ref):
    half = acc.shape[1] // 2
    g = acc[:, :half]
    u = acc[:, half:]
    o_ref[...] = (g * _sigmoid(g) * u).astype(o_ref.dtype)


def _pack_halves(y):
    c = y.shape[1] // 2
    lo = pltpu.bitcast(y[:, :c].astype(BF16).astype(F32), jnp.uint32)
    hi = pltpu.bitcast(y[:, c:].astype(BF16).astype(F32), jnp.uint32)
    return (hi & jnp.uint32(0xFFFF0000)) | (lo >> 16)


def _unpack_halves(p):
    lo = pltpu.bitcast(p << 16, F32)
    hi = pltpu.bitcast(p & jnp.uint32(0xFFFF0000), F32)
    return lo, hi


def _final_epilogue(acc, o_ref, slots_ref, wt_ref, x_ref, g_ref, fn_ref):
    half = acc.shape[1] // 2
    lo = acc[:, :half]
    hi = acc[:, half:]
    for k in range(TOP_K):
        s_lo, s_hi = _unpack_halves(slots_ref[:, k * half:(k + 1) * half])
        w = wt_ref[:, k:k + 1]
        lo = lo + w * s_lo
        hi = hi + w * s_hi
    g = g_ref[0]
    y_lo = x_ref[:, :half] + g[:, :half] * lo
    y_hi = x_ref[:, half:] + g[:, half:] * hi
    ms = (jnp.sum(y_lo * y_lo, axis=-1, keepdims=True)
          + jnp.sum(y_hi * y_hi, axis=-1, keepdims=True)) * (1.0 / (2 * half))
    r = lax.rsqrt(ms + EPS)
    o_ref[:, :half] = y_lo * r * fn_ref[:, :half]
    o_ref[:, half:] = y_hi * r * fn_ref[:, half:]


def _router_kernel(x_ref, w_ref, sh_ref, wr_ref, bias_ref, h_ref, hp_ref, ei_ref, wt_ref, cnt_ref):
    x = x_ref[...]
    ms = jnp.mean(x * x, axis=-1, keepdims=True)
    y = x * lax.rsqrt(ms + EPS) * w_ref[0] + sh_ref[0]
    h_ref[...] = y.astype(h_ref.dtype)
    hp_ref[...] = _pack_halves(y)
    logits = lax.dot_general(wr_ref[...], y, _NT, precision=lax.Precision.HIGHEST,
                             preferred_element_type=F32)
    scores = _sigmoid(logits)
    sel = scores + bias_ref[...]
    tm = sel.shape[1]
    neg = float("-inf")

    io_g = lax.broadcasted_iota(jnp.int32, (GROUP_SIZE, tm), 0).astype(F32)
    gscore = []
    for g in range(N_GROUPS):
        v = sel[g * GROUP_SIZE:(g + 1) * GROUP_SIZE, :]
        m1 = jnp.max(v, axis=0, keepdims=True)
        i1 = jnp.min(jnp.where(v == m1, io_g, float(GROUP_SIZE)), axis=0, keepdims=True)
        m2 = jnp.max(jnp.where(io_g == i1, neg, v), axis=0, keepdims=True)
        gscore.append(m1 + m2)

    parts = []
    for g in range(N_GROUPS):
        cnt = jnp.zeros((1, tm), F32)
        for g2 in range(N_GROUPS):
            if g2 == g:
                continue
            beats = (gscore[g2] >= gscore[g]) if g2 < g else (gscore[g2] > gscore[g])
            cnt = cnt + jnp.where(beats, 1.0, 0.0)
        keep = cnt < float(TOPK_GROUPS)
        parts.append(jnp.where(keep, sel[g * GROUP_SIZE:(g + 1) * GROUP_SIZE, :], neg))
    vm = jnp.concatenate(parts, axis=0)

    io = lax.broadcasted_iota(jnp.int32, (N_EXPERTS, tm), 0).astype(F32)
    idxs, ws = [], []
    picked = jnp.zeros((N_EXPERTS, tm), F32)
    for _ in range(TOP_K):
        m = jnp.max(vm, axis=0, keepdims=True)
        i = jnp.min(jnp.where(vm == m, io, float(N_EXPERTS)), axis=0, keepdims=True)
        hit = io == i
        ws.append(jnp.sum(jnp.where(hit, scores, 0.0), axis=0, keepdims=True))
        idxs.append(i)
        picked = picked + jnp.where(hit, 1.0, 0.0)
        vm = jnp.where(hit, neg, vm)
    w = jnp.concatenate(ws, axis=0)
    w = w / jnp.sum(w, axis=0, keepdims=True) * ROUTED_SCALE
    ei_ref[...] = jnp.concatenate(idxs, axis=0).astype(jnp.int32)
    wt_ref[...] = w

    part = picked[:, :LANES]
    for cch in range(1, tm // LANES):
        part = part + picked[:, cch * LANES:(cch + 1) * LANES]

    @pl.when(pl.program_id(0) == 0)
    def _():
        cnt_ref[...] = jnp.zeros_like(cnt_ref)

    cnt_ref[...] += part


def _router_call(x2, w3, sh3, w_router, bias_b, rows_per_batch, tm):
    t, d = x2.shape
    per = rows_per_batch // tm
    return pl.pallas_call(
        _router_kernel,
        grid=(t // tm,),
        in_specs=[pl.BlockSpec((tm, d), lambda i: (i, 0)),
                  pl.BlockSpec((1, 1, d), lambda i: (i // per, 0, 0)),
                  pl.BlockSpec((1, 1, d), lambda i: (i // per, 0, 0)),
                  pl.BlockSpec((N_EXPERTS, d), lambda i: (0, 0)),
                  pl.BlockSpec((N_EXPERTS, tm), lambda i: (0, 0))],
        out_specs=[pl.BlockSpec((tm, d), lambda i: (i, 0)),
                   pl.BlockSpec((tm, d // 2), lambda i: (i, 0)),
                   pl.BlockSpec((TOP_K, tm), lambda i: (0, i)),
                   pl.BlockSpec((TOP_K, tm), lambda i: (0, i)),
                   pl.BlockSpec((N_EXPERTS, LANES), lambda i: (0, 0))],
        out_shape=[jax.ShapeDtypeStruct((t, d), BF16),
                   jax.ShapeDtypeStruct((t, d // 2), jnp.uint32),
                   jax.ShapeDtypeStruct((TOP_K, t), jnp.int32),
                   jax.ShapeDtypeStruct((TOP_K, t), F32),
                   jax.ShapeDtypeStruct((N_EXPERTS, LANES), F32)],
        compiler_params=_params(("arbitrary",)),
        name="norm_router",
    )(x2, w3, sh3, w_router, bias_b)


DMA_GROUP = 8


def _for_rows(n, fn):
    n_groups = n // DMA_GROUP

    def group(gi, carry):
        for u in range(DMA_GROUP):
            fn(gi * DMA_GROUP + u)
        return carry

    def single(r, carry):
        fn(r)
        return carry

    lax.fori_loop(0, n_groups, group, 0)
    lax.fori_loop(n_groups * DMA_GROUP, n, single, 0)


def _moe_kernel(be_ref, nu_ref, i0_ref, nv_ref, order_ref,
                h_hbm, wgu_ref, wd_ref, slots_hbm,
                xbuf, ybuf, wgu_bf, wd_bf, row_view, sem_g, sem_s, *, n_tok):
    b = pl.program_id(0)
    nu = nu_ref[0]
    slot = lax.rem(b, 2)
    log_t = n_tok.bit_length() - 1

    def issue_gather(blk, sl):
        i0 = i0_ref[blk]

        def one(r):
            tok = order_ref[i0 + r] & (n_tok - 1)
            pltpu.make_async_copy(h_hbm.at[pl.ds(tok, 1)], xbuf.at[sl, pl.ds(r, 1)],
                                  sem_g.at[sl]).start()
        _for_rows(nv_ref[blk], one)

    def wait_rows(n, sem):
        pltpu.make_async_copy(row_view.at[pl.ds(0, n)], row_view.at[pl.ds(0, n)], sem).wait()

    def wait_gather(blk, sl):
        wait_rows(nv_ref[blk], sem_g.at[sl])

    def issue_scatter(blk, sl):
        i0 = i0_ref[blk]

        def one(r):
            a = order_ref[i0 + r]
            dst = (a & (n_tok - 1)) * TOP_K + (a >> log_t)
            pltpu.make_async_copy(ybuf.at[sl, pl.ds(r, 1)], slots_hbm.at[pl.ds(dst, 1)],
                                  sem_s.at[sl]).start()
        _for_rows(nv_ref[blk], one)

    def wait_scatter(blk, sl):
        wait_rows(nv_ref[blk], sem_s.at[sl])

    @pl.when(b < nu)
    def _():
        @pl.when(b == 0)
        def _():
            xbuf[...] = jnp.zeros_like(xbuf)
            issue_gather(0, 0)

        @pl.when(b + 1 < nu)
        def _():
            issue_gather(b + 1, 1 - slot)

        prev = be_ref[jnp.maximum(b - 1, 0)]

        @pl.when(jnp.logical_or(b == 0, be_ref[b] != prev))
        def _():
            wgu_bf[...] = wgu_ref[0].astype(BF16)
            wd_bf[...] = wd_ref[0].astype(BF16)

        wait_gather(b, slot)

        @pl.when(b >= 2)
        def _():
            wait_scatter(b - 2, slot)

        x_lo, x_hi = _unpack_halves(xbuf[slot])
        half = x_lo.shape[1]
        gu = (jnp.dot(x_lo.astype(BF16), wgu_bf[:half, :], preferred_element_type=F32)
              + jnp.dot(x_hi.astype(BF16), wgu_bf[half:, :], preferred_element_type=F32))
        g = gu[:, :D_EXPERT]
        u = gu[:, D_EXPERT:]
        act = (g * _sigmoid(g) * u).astype(BF16)
        y = jnp.dot(act, wd_bf[...], preferred_element_type=F32)
        ybuf[slot] = _pack_halves(y)
        issue_scatter(b, slot)

        @pl.when(b == nu - 1)
        def _():
            wait_scatter(b, slot)

            @pl.when(b >= 1)
            def _():
                wait_scatter(b - 1, 1 - slot)


def _moe_call(blk_exp, n_used, i0, nvalid, order, h2p, w_gu, w_down, n_blocks, bm):
    t, half = h2p.shape
    d = 2 * half
    grid_spec = pltpu.PrefetchScalarGridSpec(
        num_scalar_prefetch=5,
        grid=(n_blocks,),
        in_specs=[pl.BlockSpec(memory_space=pl.ANY),
                  pl.BlockSpec((1, d, 2 * D_EXPERT), lambda b, be, *_: (be[b], 0, 0)),
                  pl.BlockSpec((1, D_EXPERT, d), lambda b, be, *_: (be[b], 0, 0))],
        out_specs=pl.BlockSpec(memory_space=pl.ANY),
        scratch_shapes=[pltpu.VMEM((2, bm, half), jnp.uint32),
                        pltpu.VMEM((2, bm, half), jnp.uint32),
                        pltpu.VMEM((d, 2 * D_EXPERT), BF16),
                        pltpu.VMEM((D_EXPERT, d), BF16),
                        pltpu.VMEM((bm, half // LANES, LANES), jnp.uint32),
                        pltpu.SemaphoreType.DMA((2,)),
                        pltpu.SemaphoreType.DMA((2,))],
    )
    return pl.pallas_call(
        functools.partial(_moe_kernel, n_tok=t),
        grid_spec=grid_spec,
        out_shape=jax.ShapeDtypeStruct((t * TOP_K, half), jnp.uint32),
        compiler_params=_params(("arbitrary",)),
        name="moe_experts",
    )(blk_exp, n_used, i0, nvalid, order, h2p, w_gu, w_down)


def _dispatch(eidx, counts, bm):
    k, t = eidx.shape
    n_assign = k * t
    order = jnp.argsort(eidx.reshape(-1)).astype(jnp.int32)
    padded = (counts + bm - 1) // bm * bm
    starts = jnp.cumsum(counts) - counts
    p_ends = jnp.cumsum(padded)
    p_starts = p_ends - padded
    n_blocks = -(-(n_assign + N_EXPERTS * (bm - 1)) // bm)
    bstart = jnp.arange(n_blocks, dtype=jnp.int32) * bm
    blk_exp = jnp.minimum(jnp.searchsorted(p_ends, bstart, side="right"), N_EXPERTS - 1).astype(jnp.int32)
    j0 = bstart - p_starts[blk_exp]
    i0 = (starts[blk_exp] + j0).astype(jnp.int32)
    nvalid = jnp.clip(counts[blk_exp] - j0, 0, bm).astype(jnp.int32)
    n_used = (p_ends[-1] // bm).astype(jnp.int32).reshape(1)
    return order, blk_exp, n_used, i0, nvalid, n_blocks


def kernel(x, c, ctx, c_ctx, w_mod, b_mod, norm_attn, norm_ffn, w_in, ret_decay, diff_lambda,
           diff_subln, w_ret_o, w_diff_o, w_out, w_router, router_bias, w_exp_gu, w_exp_down,
           w_sh_gu, w_sh_down, final_norm):
    bsz, n_lat, d = x.shape
    n_ctx = ctx.shape[1]
    t = bsz * n_lat
    assert w_mod.shape[0] == 1 and d == D_MODEL
    lam_init = 0.8 - 0.6 * math.exp(-0.3 * 0)

    s = jnp.concatenate([jax.nn.silu(c), jax.nn.silu(c_ctx)[None, :],
                         jnp.zeros((8 - bsz - 1, d), F32)], axis=0)
    mod = _mod_call(s, w_mod[0], b_mod[0][None, :])
    sh1, sc1, g1, sh2, sc2, g2 = [mod[:, i * d:(i + 1) * d] for i in range(6)]
    w1 = (norm_attn[0][None, :] * (1.0 + sc1))[:, None, :]
    w2 = (norm_ffn[0][None, :] * (1.0 + sc2))[:, None, :]
    sh1 = sh1[:, None, :]
    sh2 = sh2[:, None, :]
    g1 = g1[:, None, :]
    g2 = g2[:, None, :]

    log_g = -jnp.exp(ret_decay[0].astype(F32))
    dl = diff_lambda[0].astype(F32)
    lam = (jnp.exp(jnp.sum(dl[0] * dl[1])) - jnp.exp(jnp.sum(dl[2] * dl[3])) + lam_init).reshape(1, 1)
    sub = (diff_subln[0].astype(F32) * (1.0 - lam_init))[None, :]

    x2 = x.reshape(t, d)
    h = _norm_mod_call(x2, w1[:bsz], sh1[:bsz], n_lat)
    hc = _norm_mod_call(ctx.reshape(bsz * n_ctx, d), w1[bsz:bsz + 1], sh1[bsz:bsz + 1], bsz * n_ctx,
                        tm=bsz * n_ctx)
    tabs = _rope_tables(n_lat)
    p = _proj_call(h, w_in[0], tabs, n_lat)
    tc = bsz * n_ctx
    pc = _mm_call(hc, w_in[0], _plain_epilogue, [], [],
                  jax.ShapeDtypeStruct((tc, KV_COLS), BF16),
                  pl.BlockSpec((tc, 1024), lambda j, i: (i, j)),
                  tc, 1024, "ctx_proj", n_cols=KV_COLS)
    p3 = p.reshape(bsz, n_lat, p.shape[1])
    pc3 = pc.reshape(bsz, n_ctx, KV_COLS)

    og = _ret_call(log_g, p3, pc3)

    dk0 = RK + RV
    k_att = jnp.concatenate([pc3[:, :, dk0:dk0 + DKW], p3[:, :, dk0:dk0 + DKW]], axis=1)
    v_att = jnp.concatenate([pc3[:, :, dk0 + DKW:KV_COLS], p3[:, :, dk0 + DKW:KV_COLS]], axis=1)
    od = _diff_call(lam, p3, k_att, v_att, sub)

    tm, tn = 1024, 1024
    per = n_lat // tm
    gate_blk = (KV_COLS + RK + RV + DKW) // tn
    r1 = _mm_call(og.reshape(t, RV), w_ret_o[0], _gate_epilogue, [p],
                  [pl.BlockSpec((tm, tn), lambda j, i: (i, gate_blk + j))],
                  jax.ShapeDtypeStruct((t, d), BF16),
                  pl.BlockSpec((tm, tn), lambda j, i: (i, j)), tm, tn, "ret_out")
    merged = _mm_call(od.reshape(t, DVW), w_diff_o[0], _gate_add_epilogue, [p, r1],
                      [pl.BlockSpec((tm, tn), lambda j, i: (i, gate_blk + d // tn + j)),
                       pl.BlockSpec((tm, tn), lambda j, i: (i, j))],
                      jax.ShapeDtypeStruct((t, d), BF16),
                      pl.BlockSpec((tm, tn), lambda j, i: (i, j)), tm, tn, "diff_out")
    x1 = _mm_call(merged, w_out[0], _resid_epilogue, [x2, g1[:bsz]],
                  [pl.BlockSpec((tm, tn), lambda j, i: (i, j)),
                   pl.BlockSpec((1, 1, tn), lambda j, i: (i // per, 0, j))],
                  jax.ShapeDtypeStruct((t, d), F32),
                  pl.BlockSpec((tm, tn), lambda j, i: (i, j)), tm, tn, "mix_out")

    rt = 512
    bias_b = jnp.broadcast_to(router_bias[0].astype(F32)[:, None], (N_EXPERTS, rt))
    h2, h2p, eidx, wts, cnt = _router_call(x1, w2[:bsz], sh2[:bsz], w_router[0], bias_b, n_lat, rt)

    bm = MOE_BLOCK_ROWS
    counts = jnp.sum(cnt, axis=1).astype(jnp.int32)
    order, blk_exp, n_used, i0, nvalid, n_blocks = _dispatch(eidx, counts, bm)
    slots = _moe_call(blk_exp, n_used, i0, nvalid, order, h2p, w_exp_gu[0], w_exp_down[0],
                      n_blocks, bm)
    slots_t = slots.reshape(t, TOP_K * (d // 2))

    sh_act = _mm_call(h2, w_sh_gu[0], _swiglu_epilogue, [], [],
                      jax.ShapeDtypeStruct((t, D_SHARED), BF16),
                      pl.BlockSpec((tm, D_SHARED), lambda j, i: (i, j)),
                      tm, 2 * D_SHARED, "shared_up")
    fm = 256
    per_f = n_lat // fm
    out = _mm_call(sh_act, w_sh_down[0], _final_epilogue,
                   [slots_t, wts.T, x1, g2[:bsz], final_norm[None, :]],
                   [pl.BlockSpec((fm, TOP_K * (d // 2)), lambda j, i: (i, 0)),
                    pl.BlockSpec((fm, TOP_K), lambda j, i: (i, 0)),
                    pl.BlockSpec((fm, d), lambda j, i: (i, 0)),
                    pl.BlockSpec((1, 1, d), lambda j, i: (i // per_f, 0, 0)),
                    pl.BlockSpec((1, d), lambda j, i: (0, 0))],
                   jax.ShapeDtypeStruct((t, d), F32),
                   pl.BlockSpec((fm, d), lambda j, i: (i, 0)), fm, d, "shared_down_final")
    return out.reshape(bsz, n_lat, d)
```

```python
import functools
import math

import jax
import jax.numpy as jnp
from jax import lax
from jax.experimental import pallas as pl
from jax.experimental.pallas import tpu as pltpu

F32 = jnp.float32
BF16 = jnp.bfloat16

D_MODEL = 2048
GRID_W = 64
EPS = 1e-6
ROPE_BASE = 10000.0
H_R = 8
R_DK = 128
R_DV = 256
RET_CHUNK = 128
H_D = 8
D_HEAD = 64
RK = H_R * R_DK
RV = H_R * R_DV
DKW = H_D * 2 * D_HEAD
DVW = H_D * 2 * D_HEAD
KV_COLS = RK + RV + DKW + DVW
N_EXPERTS = 256
TOP_K = 8
N_GROUPS = 8
TOPK_GROUPS = 4
GROUP_SIZE = N_EXPERTS // N_GROUPS
D_EXPERT = 256
D_SHARED = 256
ROUTED_SCALE = 2.5

VMEM_LIMIT_BYTES = 56 * 1024 * 1024
LANES = 128

MOE_BLOCK_ROWS = 256

_NT = (((1,), (1,)), ((), ()))
_TN = (((0,), (0,)), ((), ()))


def _params(sem):
    return pltpu.CompilerParams(dimension_semantics=sem, vmem_limit_bytes=VMEM_LIMIT_BYTES)


def _sigmoid(x):
    return 1.0 / (1.0 + jnp.exp(-x))


def _mod_kernel(s_ref, w_ref, b_ref, o_ref):
    acc = jnp.dot(s_ref[...].astype(BF16), w_ref[...].astype(BF16), preferred_element_type=F32)
    o_ref[...] = acc + b_ref[...]


def _mod_call(s, w, b, tn=1024):
    m, k = s.shape
    n = w.shape[1]
    return pl.pallas_call(
        _mod_kernel,
        grid=(n // tn,),
        in_specs=[pl.BlockSpec((m, k), lambda j: (0, 0)),
                  pl.BlockSpec((k, tn), lambda j: (0, j)),
                  pl.BlockSpec((1, tn), lambda j: (0, j))],
        out_specs=pl.BlockSpec((m, tn), lambda j: (0, j)),
        out_shape=jax.ShapeDtypeStruct((m, n), F32),
        compiler_params=_params(("arbitrary",)),
        name="mod_matvec",
    )(s, w, b)


def _norm_mod_kernel(x_ref, w_ref, sh_ref, o_ref):
    x = x_ref[...]
    ms = jnp.mean(x * x, axis=-1, keepdims=True)
    o_ref[...] = (x * lax.rsqrt(ms + EPS) * w_ref[0] + sh_ref[0]).astype(o_ref.dtype)


def _norm_mod_call(x2, w3, sh3, rows_per_batch, tm=512):
    t, d = x2.shape
    per = rows_per_batch // tm
    return pl.pallas_call(
        _norm_mod_kernel,
        grid=(t // tm,),
        in_specs=[pl.BlockSpec((tm, d), lambda i: (i, 0)),
                  pl.BlockSpec((1, 1, d), lambda i: (i // per, 0, 0)),
                  pl.BlockSpec((1, 1, d), lambda i: (i // per, 0, 0))],
        out_specs=pl.BlockSpec((tm, d), lambda i: (i, 0)),
        out_shape=jax.ShapeDtypeStruct((t, d), BF16),
        compiler_params=_params(("arbitrary",)),
        name="norm_mod",
    )(x2, w3, sh3)


def _mm_kernel(*refs, epilogue, n_extra):
    a_ref, w_ref = refs[0], refs[1]
    extra = refs[2:2 + n_extra]
    o_ref = refs[2 + n_extra]
    w_bf = refs[3 + n_extra]

    @pl.when(pl.program_id(1) == 0)
    def _():
        w_bf[...] = w_ref[...].astype(BF16)

    acc = jnp.dot(a_ref[...], w_bf[...], preferred_element_type=F32)
    epilogue(acc, o_ref, *extra)


def _mm_call(a, w, epilogue, extra, extra_specs, out_shape, out_spec, tm, tn, name, n_cols=None):
    t, k = a.shape
    n = w.shape[1] if n_cols is None else n_cols
    kern = functools.partial(_mm_kernel, epilogue=epilogue, n_extra=len(extra))
    return pl.pallas_call(
        kern,
        grid=(n // tn, t // tm),
        in_specs=[pl.BlockSpec((tm, k), lambda j, i: (i, 0)),
                  pl.BlockSpec((k, tn), lambda j, i: (0, j))] + list(extra_specs),
        out_specs=out_spec,
        out_shape=out_shape,
        scratch_shapes=[pltpu.VMEM((k, tn), BF16)],
        compiler_params=_params(("arbitrary", "arbitrary")),
        name=name,
    )(a, w, *extra)


def _plain_epilogue(acc, o_ref):
    o_ref[...] = acc.astype(o_ref.dtype)


_COL_RK, _COL_DK, _COL_RQ, _COL_DQ = 0, 3, 5, 8


def _rope_store(acc, o_ref, tab_ref, shift):
    c = tab_ref[0, 0]
    s_up = tab_ref[0, 1]
    s_dn = tab_ref[0, 2]
    for hh in range(acc.shape[1] // LANES):
        sl = slice(hh * LANES, (hh + 1) * LANES)
        xh = acc[:, sl]
        o = xh * c + pltpu.roll(xh, LANES - shift, 1) * s_up + pltpu.roll(xh, shift, 1) * s_dn
        o_ref[:, sl] = o.astype(o_ref.dtype)


def _proj_epilogue(acc, o_ref, tab_ref):
    j = pl.program_id(0)
    is_ret = jnp.logical_or(j == _COL_RK, j == _COL_RQ)
    is_diff = jnp.logical_or(j == _COL_DK, j == _COL_DQ)

    @pl.when(is_ret)
    def _():
        _rope_store(acc, o_ref, tab_ref, R_DK // 4)

    @pl.when(is_diff)
    def _():
        _rope_store(acc, o_ref, tab_ref, D_HEAD // 4)

    @pl.when(jnp.logical_not(jnp.logical_or(is_ret, is_diff)))
    def _():
        o_ref[...] = acc.astype(o_ref.dtype)


def _rope_tables(n_lat):
    t = jnp.arange(n_lat, dtype=jnp.int32)
    row = (t // GRID_W).astype(F32)
    col = (t % GRID_W).astype(F32)
    lane = jnp.arange(LANES, dtype=jnp.int32)

    def build(head_dim, scale):
        half_head = head_dim // 2
        half = half_head // 2
        l_in = lane % head_dim
        use_col = l_in >= half_head
        l_ax = l_in % half_head
        first = l_ax < half
        idx = (l_ax % half).astype(F32)
        inv = ROPE_BASE ** (-idx / half)
        pos = jnp.where(use_col[None, :], col[:, None], row[:, None])
        ang = pos * inv[None, :]
        cos = jnp.cos(ang)
        sin = jnp.sin(ang)
        s_up = jnp.where(first[None, :], -sin, 0.0)
        s_dn = jnp.where(first[None, :], 0.0, sin)
        return jnp.stack([cos, s_up, s_dn]) * scale

    return jnp.stack([build(R_DK, 1.0), build(R_DK, R_DK ** -0.5),
                      build(D_HEAD, 1.0), build(D_HEAD, D_HEAD ** -0.5)])


def _proj_call(h, w_in, tabs, n_lat, tm=1024, tn=1024):
    t = h.shape[0]
    n = w_in.shape[1]
    per = n_lat // tm

    def tab_idx(j, i):
        ty = jnp.where(j == _COL_RQ, 1, jnp.where(j == _COL_DK, 2, jnp.where(j == _COL_DQ, 3, 0)))
        return (ty, 0, i % per, 0)

    return _mm_call(
        h, w_in, _proj_epilogue, [tabs],
        [pl.BlockSpec((1, 3, tm, LANES), tab_idx)],
        jax.ShapeDtypeStruct((t, n), BF16),
        pl.BlockSpec((tm, tn), lambda j, i: (i, j)),
        tm, tn, "in_proj")


def _ret_kernel(logg_ref, q_ref, k_ref, v_ref, g_ref, kc_ref, vc_ref, o_ref, sb_ref):
    hh = pl.program_id(1)
    lgf = logg_ref[0, hh]
    lgb = logg_ref[1, hh]
    c_len = RET_CHUNK
    n = q_ref.shape[1]
    nc = n // c_len
    n_ctx = kc_ref.shape[1]

    ri = lax.broadcasted_iota(jnp.int32, (c_len, c_len), 0).astype(F32)
    ci = lax.broadcasted_iota(jnp.int32, (c_len, c_len), 1).astype(F32)
    dist = ri - ci
    dmat = jnp.where(dist >= 0.0,
                     jnp.exp(lgf * jnp.maximum(dist, 0.0)),
                     jnp.exp(lgb * jnp.maximum(-dist, 0.0)))
    rp = lax.broadcasted_iota(jnp.int32, (c_len, R_DK), 0).astype(F32)
    q_dec_f = jnp.exp(lgf * (rp + 1.0))
    q_dec_b = jnp.exp(lgb * (c_len - rp))
    k_dec_f = jnp.exp(lgf * (c_len - 1.0 - rp))
    k_dec_b = jnp.exp(lgb * rp)
    full = jnp.full((R_DK, R_DV), float(c_len), F32)
    c_dec_f = jnp.exp(lgf * full)
    c_dec_b = jnp.exp(lgb * full)

    mp = lax.broadcasted_iota(jnp.int32, (n_ctx, R_DK), 0).astype(F32)
    kc = kc_ref[0].astype(F32)
    vc = vc_ref[0]
    s_f0 = lax.dot_general((kc * jnp.exp(lgf * (n_ctx - 1.0 - mp))).astype(BF16), vc, _TN,
                           preferred_element_type=F32)
    s_b0 = lax.dot_general((kc * jnp.exp(lgb * mp)).astype(BF16), vc, _TN,
                           preferred_element_type=F32)

    def bwd_body(t, s):
        c = nc - 1 - t
        r0 = pl.multiple_of(c * c_len, c_len)
        sb_ref[c] = s.astype(BF16)
        kb = k_ref[0, pl.ds(r0, c_len), :].astype(F32)
        vb = v_ref[0, pl.ds(r0, c_len), :]
        upd = lax.dot_general((kb * k_dec_b).astype(BF16), vb, _TN, preferred_element_type=F32)
        return s * c_dec_b + upd

    lax.fori_loop(0, nc, bwd_body, s_b0)

    def fwd_body(c, s):
        r0 = pl.multiple_of(c * c_len, c_len)
        qb = q_ref[0, pl.ds(r0, c_len), :]
        kb = k_ref[0, pl.ds(r0, c_len), :]
        vb = v_ref[0, pl.ds(r0, c_len), :]
        sc = lax.dot_general(qb, kb, _NT, preferred_element_type=F32) * dmat
        o = jnp.dot(sc.astype(BF16), vb, preferred_element_type=F32)
        qf = qb.astype(F32)
        o = o + jnp.dot((qf * q_dec_f).astype(BF16), s.astype(BF16), preferred_element_type=F32)
        o = o + jnp.dot((qf * q_dec_b).astype(BF16), sb_ref[c], preferred_element_type=F32)
        o = o * lax.rsqrt(jnp.mean(o * o, axis=-1, keepdims=True) + EPS)
        g = g_ref[0, pl.ds(r0, c_len), :].astype(F32)
        o_ref[0, pl.ds(r0, c_len), :] = (o * (g * _sigmoid(g))).astype(o_ref.dtype)
        upd = lax.dot_general((kb.astype(F32) * k_dec_f).astype(BF16), vb, _TN,
                              preferred_element_type=F32)
        return s * c_dec_f + upd

    lax.fori_loop(0, nc, fwd_body, s_f0)


def _ret_call(log_g, p3, pc3):
    b, n, _ = p3.shape
    n_ctx = pc3.shape[1]
    nc = n // RET_CHUNK
    q_blk = (KV_COLS) // R_DK
    v_blk = RK // R_DV
    g_blk = (KV_COLS + RK) // R_DV
    return pl.pallas_call(
        _ret_kernel,
        grid=(b, H_R),
        in_specs=[pl.BlockSpec(memory_space=pltpu.SMEM),
                  pl.BlockSpec((1, n, R_DK), lambda bi, h: (bi, 0, q_blk + h)),
                  pl.BlockSpec((1, n, R_DK), lambda bi, h: (bi, 0, h)),
                  pl.BlockSpec((1, n, R_DV), lambda bi, h: (bi, 0, v_blk + h)),
                  pl.BlockSpec((1, n, R_DV), lambda bi, h: (bi, 0, g_blk + h)),
                  pl.BlockSpec((1, n_ctx, R_DK), lambda bi, h: (bi, 0, h)),
                  pl.BlockSpec((1, n_ctx, R_DV), lambda bi, h: (bi, 0, v_blk + h))],
        out_specs=pl.BlockSpec((1, n, R_DV), lambda bi, h: (bi, 0, h)),
        out_shape=jax.ShapeDtypeStruct((b, n, RV), BF16),
        scratch_shapes=[pltpu.VMEM((nc, R_DK, R_DV), BF16)],
        compiler_params=_params(("arbitrary", "arbitrary")),
        name="retention",
    )(log_g, p3, p3, p3, p3, pc3, pc3)


def _diff_kernel(lam_ref, q_ref, k_ref, v_ref, sub_ref, o_ref):
    q = q_ref[0].astype(F32)
    lane = lax.broadcasted_iota(jnp.int32, q.shape, 1)
    q0 = jnp.where(lane < D_HEAD, q, 0.0).astype(BF16)
    q1 = jnp.where(lane >= D_HEAD, q, 0.0).astype(BF16)
    k = k_ref[0]
    s0 = lax.dot_general(q0, k, _NT, preferred_element_type=F32)
    s1 = lax.dot_general(q1, k, _NT, preferred_element_type=F32)
    e0 = jnp.exp(s0 - jnp.max(s0, axis=-1, keepdims=True))
    e1 = jnp.exp(s1 - jnp.max(s1, axis=-1, keepdims=True))
    l0 = jnp.sum(e0, axis=-1, keepdims=True)
    l1 = jnp.sum(e1, axis=-1, keepdims=True)
    a = e0 * (1.0 / l0) - e1 * (lam_ref[0, 0] / l1)
    o = jnp.dot(a.astype(BF16), v_ref[0], preferred_element_type=F32)
    o = o * lax.rsqrt(jnp.mean(o * o, axis=-1, keepdims=True) + EPS) * sub_ref[...]
    o_ref[0] = o.astype(o_ref.dtype)


def _diff_call(lam, p3, k_att, v_att, sub, tq=256):
    b, n, _ = p3.shape
    m = k_att.shape[1]
    hd = 2 * D_HEAD
    q_blk = (KV_COLS + RK + RV) // hd
    return pl.pallas_call(
        _diff_kernel,
        grid=(b, H_D, n // tq),
        in_specs=[pl.BlockSpec(memory_space=pltpu.SMEM),
                  pl.BlockSpec((1, tq, hd), lambda bi, h, qi: (bi, qi, q_blk + h)),
                  pl.BlockSpec((1, m, hd), lambda bi, h, qi: (bi, 0, h)),
                  pl.BlockSpec((1, m, hd), lambda bi, h, qi: (bi, 0, h)),
                  pl.BlockSpec((1, hd), lambda bi, h, qi: (0, 0))],
        out_specs=pl.BlockSpec((1, tq, hd), lambda bi, h, qi: (bi, qi, h)),
        out_shape=jax.ShapeDtypeStruct((b, n, DVW), BF16),
        compiler_params=_params(("arbitrary", "arbitrary", "arbitrary")),
        name="diff_attn",
    )(lam, p3, k_att, v_att, sub)


def _gate_epilogue(acc, o_ref, gate_ref):
    o_ref[...] = (_sigmoid(gate_ref[...].astype(F32)) * acc).astype(o_ref.dtype)


def _gate_add_epilogue(acc, o_ref, gate_ref, prev_ref):
    o = prev_ref[...].astype(F32) + _sigmoid(gate_ref[...].astype(F32)) * acc
    o_ref[...] = o.astype(o_ref.dtype)


def _resid_epilogue(acc, o_ref, x_ref, g_ref):
    o_ref[...] = x_ref[...] + g_ref[0] * acc


def _swiglu_epilogue(acc, o_ref):
    half = acc.shape[1] // 2
    g = acc[:, :half]
    u = acc[:, half:]
    o_ref[...] = (g * _sigmoid(g) * u).astype(o_ref.dtype)


def _pack_halves(y):
    c = y.shape[1] // 2
    lo = pltpu.bitcast(y[:, :c].astype(BF16).astype(F32), jnp.uint32)
    hi = pltpu.bitcast(y[:, c:].astype(BF16).astype(F32), jnp.uint32)
    return (hi & jnp.uint32(0xFFFF0000)) | (lo >> 16)


def _unpack_halves(p):
    lo = pltpu.bitcast(p << 16, F32)
    hi = pltpu.bitcast(p & jnp.uint32(0xFFFF0000), F32)
    return lo, hi


ROW_TILES = D_MODEL // 2 // LANES


def _store_row_tiles(ref, idx, packed):
    rows = packed.shape[0]
    for c in range(ROW_TILES):
        ref[idx + (pl.ds(c, rows, stride=ROW_TILES), slice(None))] = packed[:, c * LANES:(c + 1) * LANES]


def _load_row_tiles(ref, idx, rows):
    return jnp.concatenate([ref[idx + (pl.ds(c, rows, stride=ROW_TILES), slice(None))]
                            for c in range(ROW_TILES)], axis=1)


def _final_epilogue(acc, o_ref, slots_ref, wt_ref, x_ref, g_ref, fn_ref):
    half = acc.shape[1] // 2
    lo = acc[:, :half]
    hi = acc[:, half:]
    for k in range(TOP_K):
        s_lo, s_hi = _unpack_halves(_load_row_tiles(slots_ref, (k,), acc.shape[0]))
        w = wt_ref[:, k:k + 1]
        lo = lo + w * s_lo
        hi = hi + w * s_hi
    g = g_ref[0]
    y_lo = x_ref[:, :half] + g[:, :half] * lo
    y_hi = x_ref[:, half:] + g[:, half:] * hi
    ms = (jnp.sum(y_lo * y_lo, axis=-1, keepdims=True)
          + jnp.sum(y_hi * y_hi, axis=-1, keepdims=True)) * (1.0 / (2 * half))
    r = lax.rsqrt(ms + EPS)
    o_ref[:, :half] = y_lo * r * fn_ref[:, :half]
    o_ref[:, half:] = y_hi * r * fn_ref[:, half:]


def _router_kernel(x_ref, w_ref, sh_ref, wr_ref, bias_ref, h_ref, hp_ref, ei_ref, wt_ref, cnt_ref):
    x = x_ref[...]
    ms = jnp.mean(x * x, axis=-1, keepdims=True)
    y = x * lax.rsqrt(ms + EPS) * w_ref[0] + sh_ref[0]
    h_ref[...] = y.astype(h_ref.dtype)
    _store_row_tiles(hp_ref, (), _pack_halves(y))
    logits = lax.dot_general(wr_ref[...], y, _NT, precision=lax.Precision.HIGHEST,
                             preferred_element_type=F32)
    scores = _sigmoid(logits)
    sel = scores + bias_ref[...]
    tm = sel.shape[1]
    neg = float("-inf")

    io_g = lax.broadcasted_iota(jnp.int32, (GROUP_SIZE, tm), 0).astype(F32)
    gscore = []
    for g in range(N_GROUPS):
        v = sel[g * GROUP_SIZE:(g + 1) * GROUP_SIZE, :]
        m1 = jnp.max(v, axis=0, keepdims=True)
        i1 = jnp.min(jnp.where(v == m1, io_g, float(GROUP_SIZE)), axis=0, keepdims=True)
        m2 = jnp.max(jnp.where(io_g == i1, neg, v), axis=0, keepdims=True)
        gscore.append(m1 + m2)

    parts = []
    for g in range(N_GROUPS):
        cnt = jnp.zeros((1, tm), F32)
        for g2 in range(N_GROUPS):
            if g2 == g:
                continue
            beats = (gscore[g2] >= gscore[g]) if g2 < g else (gscore[g2] > gscore[g])
            cnt = cnt + jnp.where(beats, 1.0, 0.0)
        keep = cnt < float(TOPK_GROUPS)
        parts.append(jnp.where(keep, sel[g * GROUP_SIZE:(g + 1) * GROUP_SIZE, :], neg))
    vm = jnp.concatenate(parts, axis=0)

    io = lax.broadcasted_iota(jnp.int32, (N_EXPERTS, tm), 0).astype(F32)
    idxs, ws = [], []
    picked = jnp.zeros((N_EXPERTS, tm), F32)
    for _ in range(TOP_K):
        m = jnp.max(vm, axis=0, keepdims=True)
        i = jnp.min(jnp.where(vm == m, io, float(N_EXPERTS)), axis=0, keepdims=True)
        hit = io == i
        ws.append(jnp.sum(jnp.where(hit, scores, 0.0), axis=0, keepdims=True))
        idxs.append(i)
        picked = picked + jnp.where(hit, 1.0, 0.0)
        vm = jnp.where(hit, neg, vm)
    w = jnp.concatenate(ws, axis=0)
    w = w / jnp.sum(w, axis=0, keepdims=True) * ROUTED_SCALE
    ei_ref[...] = jnp.concatenate(idxs, axis=0).astype(jnp.int32)
    wt_ref[...] = w

    part = picked[:, :LANES]
    for cch in range(1, tm // LANES):
        part = part + picked[:, cch * LANES:(cch + 1) * LANES]

    @pl.when(pl.program_id(0) == 0)
    def _():
        cnt_ref[...] = jnp.zeros_like(cnt_ref)

    cnt_ref[...] += part


def _router_call(x2, w3, sh3, w_router, bias_b, rows_per_batch, tm):
    t, d = x2.shape
    per = rows_per_batch // tm
    return pl.pallas_call(
        _router_kernel,
        grid=(t // tm,),
        in_specs=[pl.BlockSpec((tm, d), lambda i: (i, 0)),
                  pl.BlockSpec((1, 1, d), lambda i: (i // per, 0, 0)),
                  pl.BlockSpec((1, 1, d), lambda i: (i // per, 0, 0)),
                  pl.BlockSpec((N_EXPERTS, d), lambda i: (0, 0)),
                  pl.BlockSpec((N_EXPERTS, tm), lambda i: (0, 0))],
        out_specs=[pl.BlockSpec((tm, d), lambda i: (i, 0)),
                   pl.BlockSpec((tm * ROW_TILES, LANES), lambda i: (i, 0)),
                   pl.BlockSpec((TOP_K, tm), lambda i: (0, i)),
                   pl.BlockSpec((TOP_K, tm), lambda i: (0, i)),
                   pl.BlockSpec((N_EXPERTS, LANES), lambda i: (0, 0))],
        out_shape=[jax.ShapeDtypeStruct((t, d), BF16),
                   jax.ShapeDtypeStruct((t * ROW_TILES, LANES), jnp.uint32),
                   jax.ShapeDtypeStruct((TOP_K, t), jnp.int32),
                   jax.ShapeDtypeStruct((TOP_K, t), F32),
                   jax.ShapeDtypeStruct((N_EXPERTS, LANES), F32)],
        compiler_params=_params(("arbitrary",)),
        name="norm_router",
    )(x2, w3, sh3, w_router, bias_b)


DMA_GROUP = 8


def _for_rows(n, fn):
    n_groups = n // DMA_GROUP

    def group(gi, carry):
        for u in range(DMA_GROUP):
            fn(gi * DMA_GROUP + u)
        return carry

    def single(r, carry):
        fn(r)
        return carry

    lax.fori_loop(0, n_groups, group, 0)
    lax.fori_loop(n_groups * DMA_GROUP, n, single, 0)


def _moe_kernel(be_ref, nu_ref, i0_ref, nv_ref, order_ref,
                h_hbm, wgu_ref, wd_ref, slots_hbm,
                xbuf, ybuf, wgu_bf, wd_bf, sem_g, sem_s, *, n_tok):
    b = pl.program_id(0)
    nu = nu_ref[0]
    slot = lax.rem(b, 2)
    bm = xbuf.shape[1] // ROW_TILES

    def row_tile(buf, sl, r):
        return buf.at[sl, pl.ds(pl.multiple_of(r * ROW_TILES, ROW_TILES), ROW_TILES)]

    def wait_rows(hbm, n, sem):
        pltpu.make_async_copy(hbm.at[pl.ds(0, n)], hbm.at[pl.ds(0, n)], sem).wait()

    def issue_gather(blk, sl):
        i0 = i0_ref[blk]

        def one(r):
            tok = order_ref[i0 + r] & (n_tok - 1)
            pltpu.make_async_copy(h_hbm.at[tok], row_tile(xbuf, sl, r), sem_g.at[sl]).start()
        _for_rows(nv_ref[blk], one)

    def wait_gather(blk, sl):
        wait_rows(h_hbm, nv_ref[blk], sem_g.at[sl])

    def issue_scatter(blk, sl):
        i0 = i0_ref[blk]

        def one(r):
            a = order_ref[i0 + r]
            pltpu.make_async_copy(row_tile(ybuf, sl, r), slots_hbm.at[a], sem_s.at[sl]).start()
        _for_rows(nv_ref[blk], one)

    def wait_scatter(blk, sl):
        wait_rows(slots_hbm, nv_ref[blk], sem_s.at[sl])

    @pl.when(b < nu)
    def _():
        @pl.when(b == 0)
        def _():
            xbuf[...] = jnp.zeros_like(xbuf)
            issue_gather(0, 0)

        @pl.when(b + 1 < nu)
        def _():
            issue_gather(b + 1, 1 - slot)

        prev = be_ref[jnp.maximum(b - 1, 0)]

        @pl.when(jnp.logical_or(b == 0, be_ref[b] != prev))
        def _():
            wgu_bf[...] = wgu_ref[0].astype(BF16)
            wd_bf[...] = wd_ref[0].astype(BF16)

        wait_gather(b, slot)

        @pl.when(b >= 2)
        def _():
            wait_scatter(b - 2, slot)

        x_lo, x_hi = _unpack_halves(_load_row_tiles(xbuf, (slot,), bm))
        half = x_lo.shape[1]
        gu = (jnp.dot(x_lo.astype(BF16), wgu_bf[:half, :], preferred_element_type=F32)
              + jnp.dot(x_hi.astype(BF16), wgu_bf[half:, :], preferred_element_type=F32))
        g = gu[:, :D_EXPERT]
        u = gu[:, D_EXPERT:]
        act = (g * _sigmoid(g) * u).astype(BF16)
        y = jnp.dot(act, wd_bf[...], preferred_element_type=F32)
        _store_row_tiles(ybuf, (slot,), _pack_halves(y))
        issue_scatter(b, slot)

        @pl.when(b == nu - 1)
        def _():
            wait_scatter(b, slot)

            @pl.when(b >= 1)
            def _():
                wait_scatter(b - 1, 1 - slot)


def _moe_call(blk_exp, n_used, i0, nvalid, order, h2p, w_gu, w_down, n_blocks, bm):
    t = h2p.shape[0]
    d = 2 * ROW_TILES * LANES
    grid_spec = pltpu.PrefetchScalarGridSpec(
        num_scalar_prefetch=5,
        grid=(n_blocks,),
        in_specs=[pl.BlockSpec(memory_space=pl.ANY),
                  pl.BlockSpec((1, d, 2 * D_EXPERT), lambda b, be, *_: (be[b], 0, 0)),
                  pl.BlockSpec((1, D_EXPERT, d), lambda b, be, *_: (be[b], 0, 0))],
        out_specs=pl.BlockSpec(memory_space=pl.ANY),
        scratch_shapes=[pltpu.VMEM((2, bm * ROW_TILES, LANES), jnp.uint32),
                        pltpu.VMEM((2, bm * ROW_TILES, LANES), jnp.uint32),
                        pltpu.VMEM((d, 2 * D_EXPERT), BF16),
                        pltpu.VMEM((D_EXPERT, d), BF16),
                        pltpu.SemaphoreType.DMA((2,)),
                        pltpu.SemaphoreType.DMA((2,))],
    )
    return pl.pallas_call(
        functools.partial(_moe_kernel, n_tok=t),
        grid_spec=grid_spec,
        out_shape=jax.ShapeDtypeStruct((t * TOP_K, ROW_TILES, LANES), jnp.uint32),
        compiler_params=_params(("arbitrary",)),
        name="moe_experts",
    )(blk_exp, n_used, i0, nvalid, order, h2p, w_gu, w_down)


def _dispatch(eidx, counts, bm):
    k, t = eidx.shape
    n_assign = k * t
    order = jnp.argsort(eidx.reshape(-1)).astype(jnp.int32)
    padded = (counts + bm - 1) // bm * bm
    starts = jnp.cumsum(counts) - counts
    p_ends = jnp.cumsum(padded)
    p_starts = p_ends - padded
    n_blocks = -(-(n_assign + N_EXPERTS * (bm - 1)) // bm)
    bstart = jnp.arange(n_blocks, dtype=jnp.int32) * bm
    blk_exp = jnp.minimum(jnp.searchsorted(p_ends, bstart, side="right"), N_EXPERTS - 1).astype(jnp.int32)
    j0 = bstart - p_starts[blk_exp]
    i0 = (starts[blk_exp] + j0).astype(jnp.int32)
    nvalid = jnp.clip(counts[blk_exp] - j0, 0, bm).astype(jnp.int32)
    n_used = (p_ends[-1] // bm).astype(jnp.int32).reshape(1)
    return order, blk_exp, n_used, i0, nvalid, n_blocks


def kernel(x, c, ctx, c_ctx, w_mod, b_mod, norm_attn, norm_ffn, w_in, ret_decay, diff_lambda,
           diff_subln, w_ret_o, w_diff_o, w_out, w_router, router_bias, w_exp_gu, w_exp_down,
           w_sh_gu, w_sh_down, final_norm):
    bsz, n_lat, d = x.shape
    n_ctx = ctx.shape[1]
    t = bsz * n_lat
    assert w_mod.shape[0] == 1 and d == D_MODEL
    lam_init = 0.8 - 0.6 * math.exp(-0.3 * 0)

    s = jnp.concatenate([jax.nn.silu(c), jax.nn.silu(c_ctx)[None, :],
                         jnp.zeros((8 - bsz - 1, d), F32)], axis=0)
    mod = _mod_call(s, w_mod[0], b_mod[0][None, :])
    sh1, sc1, g1, sh2, sc2, g2 = [mod[:, i * d:(i + 1) * d] for i in range(6)]
    w1 = (norm_attn[0][None, :] * (1.0 + sc1))[:, None, :]
    w2 = (norm_ffn[0][None, :] * (1.0 + sc2))[:, None, :]
    sh1 = sh1[:, None, :]
    sh2 = sh2[:, None, :]
    g1 = g1[:, None, :]
    g2 = g2[:, None, :]

    log_g = -jnp.exp(ret_decay[0].astype(F32))
    dl = diff_lambda[0].astype(F32)
    lam = (jnp.exp(jnp.sum(dl[0] * dl[1])) - jnp.exp(jnp.sum(dl[2] * dl[3])) + lam_init).reshape(1, 1)
    sub = (diff_subln[0].astype(F32) * (1.0 - lam_init))[None, :]

    x2 = x.reshape(t, d)
    h = _norm_mod_call(x2, w1[:bsz], sh1[:bsz], n_lat)
    hc = _norm_mod_call(ctx.reshape(bsz * n_ctx, d), w1[bsz:bsz + 1], sh1[bsz:bsz + 1], bsz * n_ctx,
                        tm=bsz * n_ctx)
    tabs = _rope_tables(n_lat)
    p = _proj_call(h, w_in[0], tabs, n_lat)
    tc = bsz * n_ctx
    pc = _mm_call(hc, w_in[0], _plain_epilogue, [], [],
                  jax.ShapeDtypeStruct((tc, KV_COLS), BF16),
                  pl.BlockSpec((tc, 1024), lambda j, i: (i, j)),
                  tc, 1024, "ctx_proj", n_cols=KV_COLS)
    p3 = p.reshape(bsz, n_lat, p.shape[1])
    pc3 = pc.reshape(bsz, n_ctx, KV_COLS)

    og = _ret_call(log_g, p3, pc3)

    dk0 = RK + RV
    k_att = jnp.concatenate([pc3[:, :, dk0:dk0 + DKW], p3[:, :, dk0:dk0 + DKW]], axis=1)
    v_att = jnp.concatenate([pc3[:, :, dk0 + DKW:KV_COLS], p3[:, :, dk0 + DKW:KV_COLS]], axis=1)
    od = _diff_call(lam, p3, k_att, v_att, sub)

    tm, tn = 1024, 1024
    per = n_lat // tm
    gate_blk = (KV_COLS + RK + RV + DKW) // tn
    r1 = _mm_call(og.reshape(t, RV), w_ret_o[0], _gate_epilogue, [p],
                  [pl.BlockSpec((tm, tn), lambda j, i: (i, gate_blk + j))],
                  jax.ShapeDtypeStruct((t, d), BF16),
                  pl.BlockSpec((tm, tn), lambda j, i: (i, j)), tm, tn, "ret_out")
    merged = _mm_call(od.reshape(t, DVW), w_diff_o[0], _gate_add_epilogue, [p, r1],
                      [pl.BlockSpec((tm, tn), lambda j, i: (i, gate_blk + d // tn + j)),
                       pl.BlockSpec((tm, tn), lambda j, i: (i, j))],
                      jax.ShapeDtypeStruct((t, d), BF16),
                      pl.BlockSpec((tm, tn), lambda j, i: (i, j)), tm, tn, "diff_out")
    x1 = _mm_call(merged, w_out[0], _resid_epilogue, [x2, g1[:bsz]],
                  [pl.BlockSpec((tm, tn), lambda j, i: (i, j)),
                   pl.BlockSpec((1, 1, tn), lambda j, i: (i // per, 0, j))],
                  jax.ShapeDtypeStruct((t, d), F32),
                  pl.BlockSpec((tm, tn), lambda j, i: (i, j)), tm, tn, "mix_out")

    rt = 512
    bias_b = jnp.broadcast_to(router_bias[0].astype(F32)[:, None], (N_EXPERTS, rt))
    h2, h2p, eidx, wts, cnt = _router_call(x1, w2[:bsz], sh2[:bsz], w_router[0], bias_b, n_lat, rt)

    bm = MOE_BLOCK_ROWS
    counts = jnp.sum(cnt, axis=1).astype(jnp.int32)
    order, blk_exp, n_used, i0, nvalid, n_blocks = _dispatch(eidx, counts, bm)
    slots = _moe_call(blk_exp, n_used, i0, nvalid, order, h2p.reshape(t, ROW_TILES, LANES),
                      w_exp_gu[0], w_exp_down[0], n_blocks, bm)
    slots_k = slots.reshape(TOP_K, t * ROW_TILES, LANES)

    sh_act = _mm_call(h2, w_sh_gu[0], _swiglu_epilogue, [], [],
                      jax.ShapeDtypeStruct((t, D_SHARED), BF16),
                      pl.BlockSpec((tm, D_SHARED), lambda j, i: (i, j)),
                      tm, 2 * D_SHARED, "shared_up")
    fm = 256
    per_f = n_lat // fm
    out = _mm_call(sh_act, w_sh_down[0], _final_epilogue,
                   [slots_k, wts.T, x1, g2[:bsz], final_norm[None, :]],
                   [pl.BlockSpec((TOP_K, fm * ROW_TILES, LANES), lambda j, i: (0, i, 0)),
                    pl.BlockSpec((fm, TOP_K), lambda j, i: (i, 0)),
                    pl.BlockSpec((fm, d), lambda j, i: (i, 0)),
                    pl.BlockSpec((1, 1, d), lambda j, i: (i // per_f, 0, 0)),
                    pl.BlockSpec((1, d), lambda j, i: (0, 0))],
                   jax.ShapeDtypeStruct((t, d), F32),
                   pl.BlockSpec((fm, d), lambda j, i: (i, 0)), fm, d, "shared_down_final")
    return out.reshape(bsz, n_lat, d)
```
